```python
import math
import functools
import jax
import jax.numpy as jnp
from jax import lax
import numpy as np

D_MODEL = 1024
BATCH = 8
SEQ = 2048
DEPTH = 2
DEC_BATCH = 32
DEC_SEQ = 4
PAST_LEN = 8192
PAGE_SIZE = 128

N_HEADS = 8
N_KV = 2
HEAD_DIM = 64
GROUP = N_HEADS // N_KV
CMP_LEN = 32
CMP_STRIDE = 16
SEL_BLOCK = 64
N_SEL = 16
WINDOW = 512
Q_BLOCK = 128
ROPE_THETA = 10000.0
SSM_HEADS = 8
SSM_HEAD_DIM = 64
SSM_INNER = SSM_HEADS * SSM_HEAD_DIM
SSM_GROUPS = 2
SSM_STATE = 64
SSM_CONV = 4
SSM_CONV_DIM = SSM_INNER + 2 * SSM_GROUPS * SSM_STATE
SSM_CHUNK = 128
SC_DIM = 512
SC_WIDTH = 3
N_BRANCH = 3
BRANCH_DIM = 512
D_FF = 2816
N_EXPERTS = 8
TOP_K = 2
D_FF_EXPERT = 3584
N_DENSE = (DEPTH + 1) // 2
N_MOE = DEPTH // 2
EPS = 1e-6
NEG_INF = -1e30
FORCED_SCORE = 1e4
Q_DIM = N_HEADS * HEAD_DIM
KV_DIM = N_KV * HEAD_DIM
SPLITS = (Q_DIM, 6 * KV_DIM, 3 * N_HEADS, SSM_INNER, SSM_CONV_DIM, SSM_HEADS, SC_DIM, SC_DIM, SC_DIM, N_BRANCH * D_MODEL)
IN_DIM = sum(SPLITS)

kernel_name = 'nsa_ssd_shortconv_gated_hybrid_step'


def rms_norm(x, w):
    xf = x.astype(jnp.float32)
    y = xf * lax.rsqrt(jnp.mean(xf * xf, axis=-1, keepdims=True) + EPS)
    return (y * w.astype(jnp.float32)).astype(x.dtype)


def rope(x, pos):
    half = HEAD_DIM // 2
    inv_freq = jnp.exp(-math.log(ROPE_THETA) * jnp.arange(half, dtype=jnp.float32) / half)
    ang = pos.astype(jnp.float32)[:, None] * inv_freq[None, :]
    shape = (1, pos.shape[0]) + (1,) * (x.ndim - 3) + (half,)
    cos = jnp.cos(ang).reshape(shape)
    sin = jnp.sin(ang).reshape(shape)
    xf = x.astype(jnp.float32)
    x1, x2 = xf[..., :half], xf[..., half:]
    return jnp.concatenate([x1 * cos - x2 * sin, x2 * cos + x1 * sin], axis=-1).astype(x.dtype)


def masked_softmax(s, mask):
    p = jax.nn.softmax(jnp.where(mask, s.astype(jnp.float32), NEG_INF), axis=-1)
    return jnp.where(mask, p, 0.0)


def causal_dwconv(x, buf, w, bias=None):
    width, t = w.shape[0], x.shape[1]
    xp = jnp.concatenate([buf.astype(x.dtype), x], axis=1)
    y = xp[:, 0:t] * w[0]
    for k in range(1, width):
        y = y + xp[:, k:k + t] * w[k]
    if bias is not None:
        y = y + bias
    return y, xp[:, t:]


def ssd_scan(x, dt, a, bm, cm, h0, chunk):
    bsz, t = x.shape[:2]
    nc = t // chunk

    def to_chunks(v):
        return jnp.moveaxis(v.reshape((bsz, nc, chunk) + v.shape[2:]), 1, 0)

    causal = jnp.tril(jnp.ones((chunk, chunk), dtype=bool))

    def step(h, inp):
        xc, dtc, bc, cc = inp
        acum = jnp.cumsum(dtc * a, axis=1)
        seg = acum[:, :, None, :] - acum[:, None, :, :]
        decay = jnp.exp(jnp.where(causal[None, :, :, None], seg, NEG_INF))
        xdt = xc * dtc[..., None]
        cb = jnp.einsum('bthn,bshn->btsh', cc, bc)
        y_diag = jnp.einsum('btsh,bshp->bthp', cb * decay, xdt)
        y_off = jnp.einsum('bthn,bhpn->bthp', cc, h) * jnp.exp(acum)[..., None]
        tail = jnp.exp(acum[:, -1:, :] - acum)
        h_new = h * jnp.exp(acum[:, -1])[:, :, None, None] + jnp.einsum('bshp,bshn->bhpn', xdt * tail[..., None], bc)
        return h_new, y_diag + y_off

    h_t, ys = lax.scan(step, h0, (to_chunks(x), to_chunks(dt), to_chunks(bm), to_chunks(cm)))
    return jnp.moveaxis(ys, 0, 1).reshape(x.shape), h_t


def nsa_attention(q, q_pos, rows, win, w_cmp):
    bsz, t = q.shape[:2]
    seq_len = rows.shape[1]
    n_blk = -(-seq_len // SEL_BLOCK)
    l_pad = n_blk * SEL_BLOCK
    rows = jnp.pad(rows, ((0, 0), (0, l_pad - seq_len), (0, 0), (0, 0), (0, 0)))
    w_halves = w_cmp.reshape(2, CMP_LEN // CMP_STRIDE, CMP_STRIDE, HEAD_DIM, HEAD_DIM)
    strides = rows.reshape(bsz, l_pad // CMP_STRIDE, CMP_STRIDE, 4, N_KV, HEAD_DIM)

    def compress(r, w):
        return jnp.einsum('bcjgd,jde->bcge', r[:, :-1], w[0]) + jnp.einsum('bcjgd,jde->bcge', r[:, 1:], w[1])

    k_cmp = compress(strides[:, :, :, 0], w_halves[0])
    v_cmp = compress(strides[:, :, :, 1], w_halves[1])
    n_cmp = k_cmp.shape[1]
    cmp_end = jnp.arange(n_cmp) * CMP_STRIDE + (CMP_LEN - 1)
    ratio = SEL_BLOCK // CMP_STRIDE
    sel = rows[:, :, 2:4].reshape(bsz, n_blk, SEL_BLOCK, 2, N_KV, HEAD_DIM).transpose(0, 3, 4, 1, 2, 5)
    k_sel = min(N_SEL, n_blk)
    blk = jnp.arange(n_blk)
    g_ix = jnp.arange(N_KV)[None, :, None]
    lw = win.shape[1]
    win_pad = jnp.pad(win, ((0, 0), (WINDOW, 0), (0, 0), (0, 0), (0, 0)))
    idx_w = jnp.arange(WINDOW + lw)
    wpos = q_pos[0] - (lw - t) - WINDOW + idx_w
    wreal = idx_w >= WINDOW
    qb = math.gcd(t, Q_BLOCK)
    n_qb = t // qb
    scale = HEAD_DIM ** -0.5

    def one_block(item):
        b, qi, qp, ws = item
        s_c = jnp.einsum('qgrd,cgd->qgrc', qi, k_cmp[b]) * scale
        p_c = masked_softmax(s_c, (cmp_end[None, :] <= qp[:, None])[:, None, None, :])
        o_c = jnp.einsum('qgrc,cgd->qgrd', p_c, v_cmp[b])
        imp = jnp.pad(p_c.sum(axis=2), ((0, 0), (0, 0), (0, n_blk * ratio - n_cmp)))
        imp = imp.reshape(qb, N_KV, n_blk, ratio).sum(-1)
        cur = qp[:, None] // SEL_BLOCK
        valid = blk[None, :] <= cur
        forced = (blk[None, :] == 0) | (blk[None, :] == cur) | (blk[None, :] == cur - 1)
        score = jnp.where(valid[:, None], jnp.where(forced[:, None], FORCED_SCORE, imp), -1.0)
        top_s, top_i = lax.top_k(score, k_sel)
        sel_b = sel[b]
        k_g = sel_b[0][g_ix, top_i]
        v_g = sel_b[1][g_ix, top_i]
        kpos = top_i[..., None] * SEL_BLOCK + jnp.arange(SEL_BLOCK)
        m_s = (top_s >= 0.0)[..., None] & (kpos <= qp[:, None, None, None])
        s_s = jnp.einsum('qgrd,qgksd->qgrks', qi, k_g).reshape(qb, N_KV, GROUP, k_sel * SEL_BLOCK) * scale
        p_s = masked_softmax(s_s, m_s.reshape(qb, N_KV, 1, k_sel * SEL_BLOCK))
        o_s = jnp.einsum('qgrm,qgmd->qgrd', p_s, v_g.reshape(qb, N_KV, k_sel * SEL_BLOCK, HEAD_DIM))
        wb = lax.dynamic_slice_in_dim(win_pad[b], ws, WINDOW + qb, axis=0)
        kp = lax.dynamic_slice_in_dim(wpos, ws, WINDOW + qb)
        kr = lax.dynamic_slice_in_dim(wreal, ws, WINDOW + qb)
        dpos = qp[:, None] - kp[None, :]
        m_w = kr[None, :] & (dpos >= 0) & (dpos < WINDOW)
        s_w = jnp.einsum('qgrd,kgd->qgrk', qi, wb[:, 0]) * scale
        p_w = masked_softmax(s_w, m_w[:, None, None, :])
        o_w = jnp.einsum('qgrk,kgd->qgrd', p_w, wb[:, 1])
        return jnp.stack([o_c, o_s, o_w], axis=1).astype(qi.dtype)

    b_idx = jnp.repeat(jnp.arange(bsz), n_qb)
    i_idx = jnp.tile(jnp.arange(n_qb), bsz)
    q_items = q.reshape(bsz * n_qb, qb, N_KV, GROUP, HEAD_DIM)
    qp_items = q_pos.reshape(n_qb, qb)[i_idx]
    ws_items = (lw - t) + i_idx * qb
    out = lax.map(one_block, (b_idx, q_items, qp_items, ws_items))
    return out.reshape(bsz, t, 3, N_HEADS, HEAD_DIM)


def token_mixers(h, pos, past_rows, win_buf, win_keep, ssm_h0, ssm_buf, sc_buf,
                 w_in, q_norm, k_norm, w_cmp, ssm_conv_w, ssm_conv_b, ssm_dt_bias,
                 ssm_a_log, ssm_d, ssm_norm, sc_conv_w, w_branch, w_out):
    bsz, t = h.shape[:2]
    offs = np.cumsum(SPLITS)[:-1].tolist()
    (q, kv6, nsa_g, ssm_z, ssm_xbc, ssm_dt, sc_b, sc_c, sc_h, merge_g) = jnp.split(h @ w_in, offs, axis=-1)

    q = rope(rms_norm(q.reshape(bsz, t, N_HEADS, HEAD_DIM), q_norm), pos)
    kv6 = kv6.reshape(bsz, t, 3, 2, N_KV, HEAD_DIM)
    keys = rope(rms_norm(kv6[:, :, :, 0], k_norm[:, None, :]), pos)
    kv = jnp.stack([keys, kv6[:, :, :, 1]], axis=3)
    new_rows = kv[:, :, :2].reshape(bsz, t, 4, N_KV, HEAD_DIM)
    rows = jnp.concatenate([past_rows, new_rows.astype(past_rows.dtype)], axis=1)
    win = jnp.concatenate([win_buf, kv[:, :, 2].astype(win_buf.dtype)], axis=1)
    o_br = nsa_attention(q.reshape(bsz, t, N_KV, GROUP, HEAD_DIM), pos, rows, win, w_cmp)
    g_nsa = jax.nn.sigmoid(nsa_g.reshape(bsz, t, N_HEADS, 3))
    o_nsa = jnp.einsum('bthk,btkhd->bthd', g_nsa, o_br).reshape(bsz, t, Q_DIM)
    new_win = win[:, win.shape[1] - win_keep:]

    xbc, new_ssm_buf = causal_dwconv(ssm_xbc, ssm_buf, ssm_conv_w, ssm_conv_b)
    xbc = jax.nn.silu(xbc)
    xs, bm, cm = jnp.split(xbc, [SSM_INNER, SSM_INNER + SSM_GROUPS * SSM_STATE], axis=-1)
    rep = SSM_HEADS // SSM_GROUPS
    xs = xs.reshape(bsz, t, SSM_HEADS, SSM_HEAD_DIM).astype(jnp.float32)
    bm = jnp.repeat(bm.reshape(bsz, t, SSM_GROUPS, SSM_STATE), rep, axis=2).astype(jnp.float32)
    cm = jnp.repeat(cm.reshape(bsz, t, SSM_GROUPS, SSM_STATE), rep, axis=2).astype(jnp.float32)
    dt = jax.nn.softplus(ssm_dt.astype(jnp.float32) + ssm_dt_bias.astype(jnp.float32))
    a = -jnp.exp(ssm_a_log.astype(jnp.float32))
    y_ssm, h_t = ssd_scan(xs, dt, a, bm, cm, ssm_h0.astype(jnp.float32), math.gcd(t, SSM_CHUNK))
    y_ssm = y_ssm + xs * ssm_d.astype(jnp.float32)[:, None]
    y_ssm = y_ssm.reshape(bsz, t, SSM_INNER) * jax.nn.silu(ssm_z.astype(jnp.float32))
    o_ssm = rms_norm(y_ssm, ssm_norm).astype(h.dtype)

    sc_out, new_sc_buf = causal_dwconv(sc_c * sc_h, sc_buf, sc_conv_w)
    o_sc = sc_b * sc_out

    o_all = jnp.stack([o_nsa, o_ssm, o_sc.astype(o_nsa.dtype)], axis=2)
    proj = jnp.einsum('btkc,kcd->btkd', o_all, w_branch)
    gate = jax.nn.sigmoid(merge_g.reshape(bsz, t, N_BRANCH, D_MODEL))
    y = jnp.einsum('btkd,btkd->btd', gate, proj) @ w_out
    return y, new_rows, new_win, h_t.astype(h.dtype), new_ssm_buf, new_sc_buf


def swiglu(h, w_gate, w_up, w_down):
    return (jax.nn.silu(h @ w_gate) * (h @ w_up)) @ w_down


def moe_swiglu(h, router, w_gate, w_up, w_down):
    logits = jnp.einsum('btd,de->bte', h, router).astype(jnp.float32)
    top_v, top_i = lax.top_k(logits, TOP_K)
    gate = jax.nn.softmax(top_v, axis=-1)
    combine = jnp.einsum('btk,btke->bte', gate, jax.nn.one_hot(top_i, N_EXPERTS, dtype=jnp.float32))
    out = jnp.zeros(h.shape, jnp.float32)
    for e in range(N_EXPERTS):
        he = jax.nn.silu(h @ w_gate[e]) * (h @ w_up[e])
        out = out + combine[..., e:e + 1] * (he @ w_down[e])
    return out.astype(h.dtype)


def setup_inputs(seed: int = 0) -> dict:
    key = jax.random.key(seed)
    ks = list(jax.random.split(key, 32))
    f32 = jnp.float32

    def nrm(k, shape, scale):
        return jax.random.normal(k, shape, f32) * scale

    n_pages = PAST_LEN // PAGE_SIZE
    n_used = DEC_BATCH * n_pages
    n_pool = n_used + n_used // 4
    win_buf = min(WINDOW, PAST_LEN)
    page_table = jax.random.permutation(ks[7], n_pool)[:n_used].reshape(DEC_BATCH, n_pages).astype(jnp.int32)
    dt0 = jnp.exp(jax.random.uniform(ks[15], (DEPTH, SSM_HEADS), f32, math.log(1e-3), math.log(1e-1)))
    return {
        'x_prompt': nrm(ks[0], (BATCH, SEQ, D_MODEL), 1.0),
        'x_sample': nrm(ks[1], (DEC_BATCH, DEC_SEQ, D_MODEL), 1.0),
        'cache_nsa_pages': nrm(ks[2], (DEPTH, n_pool, PAGE_SIZE, 4, N_KV, HEAD_DIM), 1.0),
        'cache_nsa_window': nrm(ks[3], (DEPTH, DEC_BATCH, win_buf, 2, N_KV, HEAD_DIM), 1.0),
        'state_ssm': nrm(ks[4], (DEPTH, DEC_BATCH, SSM_HEADS, SSM_HEAD_DIM, SSM_STATE), 0.1),
        'state_ssm_conv': nrm(ks[5], (DEPTH, DEC_BATCH, SSM_CONV - 1, SSM_CONV_DIM), 1.0),
        'state_shortconv': nrm(ks[6], (DEPTH, DEC_BATCH, SC_WIDTH - 1, SC_DIM), 1.0),
        'page_table': page_table,
        'norm_mix': 1.0 + nrm(ks[8], (DEPTH, D_MODEL), 0.02),
        'w_in': nrm(ks[9], (DEPTH, D_MODEL, IN_DIM), D_MODEL ** -0.5),
        'q_norm': 1.0 + nrm(ks[10], (DEPTH, HEAD_DIM), 0.02),
        'k_norm': 1.0 + nrm(ks[11], (DEPTH, 3, HEAD_DIM), 0.02),
        'w_cmp': nrm(ks[12], (DEPTH, 2, CMP_LEN, HEAD_DIM, HEAD_DIM), (CMP_LEN * HEAD_DIM) ** -0.5),
        'ssm_conv_w': nrm(ks[13], (DEPTH, SSM_CONV, SSM_CONV_DIM), SSM_CONV ** -0.5),
        'ssm_conv_b': nrm(ks[14], (DEPTH, SSM_CONV_DIM), 0.01),
        'ssm_dt_bias': dt0 + jnp.log(-jnp.expm1(-dt0)),
        'ssm_a_log': jnp.log(jax.random.uniform(ks[16], (DEPTH, SSM_HEADS), f32, 1.0, 16.0)),
        'ssm_d': 1.0 + nrm(ks[17], (DEPTH, SSM_HEADS), 0.02),
        'ssm_norm': 1.0 + nrm(ks[18], (DEPTH, SSM_INNER), 0.02),
        'sc_conv_w': nrm(ks[19], (DEPTH, SC_WIDTH, SC_DIM), SC_WIDTH ** -0.5),
        'w_branch': nrm(ks[20], (DEPTH, N_BRANCH, BRANCH_DIM, D_MODEL), BRANCH_DIM ** -0.5),
        'w_out': nrm(ks[21], (DEPTH, D_MODEL, D_MODEL), D_MODEL ** -0.5),
        'norm_ffn': 1.0 + nrm(ks[22], (DEPTH, D_MODEL), 0.02),
        'ffn_w_gate': nrm(ks[23], (N_DENSE, D_MODEL, D_FF), D_MODEL ** -0.5),
        'ffn_w_up': nrm(ks[24], (N_DENSE, D_MODEL, D_FF), D_MODEL ** -0.5),
        'ffn_w_down': nrm(ks[25], (N_DENSE, D_FF, D_MODEL), D_FF ** -0.5),
        'moe_router': nrm(ks[26], (N_MOE, D_MODEL, N_EXPERTS), D_MODEL ** -0.5),
        'moe_w_gate': nrm(ks[27], (N_MOE, N_EXPERTS, D_MODEL, D_FF_EXPERT), D_MODEL ** -0.5),
        'moe_w_up': nrm(ks[28], (N_MOE, N_EXPERTS, D_MODEL, D_FF_EXPERT), D_MODEL ** -0.5),
        'moe_w_down': nrm(ks[29], (N_MOE, N_EXPERTS, D_FF_EXPERT, D_MODEL), D_FF_EXPERT ** -0.5),
    }


def reference(x_prompt, x_sample, cache_nsa_pages, cache_nsa_window, state_ssm, state_ssm_conv,
              state_shortconv, page_table, norm_mix, w_in, q_norm, k_norm, w_cmp, ssm_conv_w,
              ssm_conv_b, ssm_dt_bias, ssm_a_log, ssm_d, ssm_norm, sc_conv_w, w_branch, w_out,
              norm_ffn, ffn_w_gate, ffn_w_up, ffn_w_down, moe_router, moe_w_gate, moe_w_up, moe_w_down):
    bp, tp = x_prompt.shape[:2]
    bd, td = x_sample.shape[:2]
    past_len = page_table.shape[1] * cache_nsa_pages.shape[2]
    pos_p = jnp.arange(tp, dtype=jnp.int32)
    pos_s = past_len + jnp.arange(td, dtype=jnp.int32)
    dtp = x_prompt.dtype
    zero_rows = jnp.zeros((bp, 0, 4, N_KV, HEAD_DIM), dtp)
    zero_win = jnp.zeros((bp, 0, 2, N_KV, HEAD_DIM), dtp)
    zero_ssm = jnp.zeros((bp, SSM_HEADS, SSM_HEAD_DIM, SSM_STATE), dtp)
    zero_ssm_conv = jnp.zeros((bp, SSM_CONV - 1, SSM_CONV_DIM), dtp)
    zero_sc = jnp.zeros((bp, SC_WIDTH - 1, SC_DIM), dtp)
    xp, xs = x_prompt, x_sample
    rows_p, rows_s, win_p, win_s, ssm_p, ssm_s = [], [], [], [], [], []
    sconv_p, sconv_s, cconv_p, cconv_s = [], [], [], []
    for l in range(DEPTH):
        mix = functools.partial(
            token_mixers, w_in=w_in[l], q_norm=q_norm[l], k_norm=k_norm[l], w_cmp=w_cmp[l],
            ssm_conv_w=ssm_conv_w[l], ssm_conv_b=ssm_conv_b[l], ssm_dt_bias=ssm_dt_bias[l],
            ssm_a_log=ssm_a_log[l], ssm_d=ssm_d[l], ssm_norm=ssm_norm[l], sc_conv_w=sc_conv_w[l],
            w_branch=w_branch[l], w_out=w_out[l])
        yp, r_p, w_p, h_p, b_p, c_p = mix(rms_norm(xp, norm_mix[l]), pos_p, zero_rows, zero_win,
                                          min(WINDOW, tp), zero_ssm, zero_ssm_conv, zero_sc)
        xp = xp + yp
        past = cache_nsa_pages[l, page_table].reshape(bd, past_len, 4, N_KV, HEAD_DIM)
        ys, r_s, w_s, h_s, b_s, c_s = mix(rms_norm(xs, norm_mix[l]), pos_s, past, cache_nsa_window[l],
                                          cache_nsa_window.shape[2], state_ssm[l], state_ssm_conv[l],
                                          state_shortconv[l])
        xs = xs + ys
        if l % 2 == 0:
            ffn = functools.partial(swiglu, w_gate=ffn_w_gate[l // 2], w_up=ffn_w_up[l // 2],
                                    w_down=ffn_w_down[l // 2])
        else:
            ffn = functools.partial(moe_swiglu, router=moe_router[l // 2], w_gate=moe_w_gate[l // 2],
                                    w_up=moe_w_up[l // 2], w_down=moe_w_down[l // 2])
        xp = xp + ffn(rms_norm(xp, norm_ffn[l]))
        xs = xs + ffn(rms_norm(xs, norm_ffn[l]))
        rows_p.append(r_p); rows_s.append(r_s); win_p.append(w_p); win_s.append(w_s)
        ssm_p.append(h_p); ssm_s.append(h_s); sconv_p.append(b_p); sconv_s.append(b_s)
        cconv_p.append(c_p); cconv_s.append(c_s)
    y_prompt, y_sample = xp, xs
    nsa_rows_prompt, nsa_rows_sample = jnp.stack(rows_p), jnp.stack(rows_s)
    nsa_window_prompt, nsa_window_sample = jnp.stack(win_p), jnp.stack(win_s)
    ssm_prompt, ssm_sample = jnp.stack(ssm_p), jnp.stack(ssm_s)
    ssm_conv_prompt, ssm_conv_sample = jnp.stack(sconv_p), jnp.stack(sconv_s)
    shortconv_prompt, shortconv_sample = jnp.stack(cconv_p), jnp.stack(cconv_s)
    return (y_prompt, y_sample, nsa_rows_prompt, nsa_rows_sample, nsa_window_prompt, nsa_window_sample,
            ssm_prompt, ssm_sample, ssm_conv_prompt, ssm_conv_sample, shortconv_prompt, shortconv_sample)
```

```python
import functools
import math

import jax
import jax.numpy as jnp
from jax import lax
from jax.experimental import pallas as pl
from jax.experimental.pallas import tpu as pltpu

F32 = jnp.float32
BF16 = jnp.bfloat16
HIGHEST = lax.Precision.HIGHEST

D_MODEL = 1024
N_HEADS = 8
N_KV = 2
HEAD_DIM = 64
GROUP = N_HEADS // N_KV
CMP_LEN = 32
CMP_STRIDE = 16
SEL_BLOCK = 64
N_SEL = 16
WINDOW = 512
ROPE_THETA = 10000.0
SSM_HEADS = 8
SSM_HEAD_DIM = 64
SSM_INNER = 512
SSM_GROUPS = 2
SSM_STATE = 64
SSM_CONV = 4
SSM_CONV_DIM = 768
SC_DIM = 512
SC_WIDTH = 3
N_BRANCH = 3
BRANCH_DIM = 512
N_EXPERTS = 8
TOP_K = 2
EPS = 1e-6
NEG_INF = -1e30
FORCED_SCORE = 1e4
Q_DIM = 512
KV_DIM = 128

LANES = 128
CHUNK = 128
SAMPLE_SLOT = 8
VMEM_LIMIT = 56 * 1024 * 1024

Q0, Z0, SCB0, KV0, XBC0, SCC0, SCH0, MG0, SM0, N_PROJ = 0, 512, 1024, 1536, 2304, 3072, 3584, 4096, 7168, 7680
GATE_LANE0 = 0
DT_LANE0 = 24
_OQ, _OKV, _OG, _OZ, _OXBC, _ODT, _OSCB, _OSCC, _OSCH, _OMG = 0, 512, 1280, 1304, 1816, 2584, 2592, 3104, 3616, 4128


def _cparams(sem):
    return pltpu.CompilerParams(dimension_semantics=sem, vmem_limit_bytes=VMEM_LIMIT)


def _pick(n, cands):
    for c in cands:
        if n % c == 0:
            return c
    raise ValueError(f"no tile for {n}")


def _rms(x, w):
    ms = jnp.mean(x * x, axis=-1, keepdims=True)
    return (x * lax.rsqrt(ms + EPS)) * w


def _silu(x):
    return x * jax.nn.sigmoid(x)


def _norm_matmul_body(x_ref, g_ref, w_ref, o_ref, xn_ref):
    @pl.when(pl.program_id(1) == 0)
    def _():
        xn_ref[...] = _rms(x_ref[...], g_ref[...]).astype(BF16)

    o_ref[...] = jnp.dot(xn_ref[...], w_ref[...], preferred_element_type=F32)


def norm_matmul(x, g, w):
    t, d = x.shape
    n = w.shape[1]
    tm = _pick(t, (1280, 1024, 640, 512, 384, 256, 128))
    tn = _pick(n, (1536, 1280, 1024, 512, 256, 128))
    return pl.pallas_call(
        _norm_matmul_body,
        grid=(t // tm, n // tn),
        in_specs=[pl.BlockSpec((tm, d), lambda i, j: (i, 0)),
                  pl.BlockSpec((1, d), lambda i, j: (0, 0)),
                  pl.BlockSpec((d, tn), lambda i, j: (0, j))],
        out_specs=pl.BlockSpec((tm, tn), lambda i, j: (i, j)),
        out_shape=jax.ShapeDtypeStruct((t, n), F32),
        scratch_shapes=[pltpu.VMEM((tm, d), BF16)],
        compiler_params=_cparams(("parallel", "arbitrary")),
        name="norm_matmul",
    )(x, g.reshape(1, d), w)


def _prep_body(q_ref, kv_ref, cos_ref, sin_ref, qw_ref, kw_ref, bd_ref, qo_ref, rows_ref, win_ref):
    cos = cos_ref[...]
    sin = sin_ref[...]
    bd = bd_ref[...]
    lane = lax.broadcasted_iota(jnp.int32, cos.shape, 1)
    first_half = (lane % HEAD_DIM) < (HEAD_DIM // 2)

    def norm_rope(x, w):
        ms = jnp.dot(x * x, bd, precision=HIGHEST, preferred_element_type=F32)
        y = (x * lax.rsqrt(ms + EPS)) * w
        partner = jnp.where(first_half, pltpu.roll(y, LANES - HEAD_DIM // 2, 1), pltpu.roll(y, HEAD_DIM // 2, 1))
        return y * cos + partner * sin

    scale = HEAD_DIM ** -0.5
    for c in range(Q_DIM // LANES):
        sl = slice(c * LANES, (c + 1) * LANES)
        qo_ref[:, sl] = (norm_rope(q_ref[:, sl], qw_ref[...]) * scale).astype(BF16)
    for br in range(3):
        k = norm_rope(kv_ref[:, (2 * br) * LANES:(2 * br + 1) * LANES], kw_ref[br:br + 1, :])
        v = kv_ref[:, (2 * br + 1) * LANES:(2 * br + 2) * LANES]
        if br < 2:
            rows_ref[:, (2 * br) * LANES:(2 * br + 1) * LANES] = k
            rows_ref[:, (2 * br + 1) * LANES:(2 * br + 2) * LANES] = v
        else:
            win_ref[:, 0:LANES] = k
            win_ref[:, LANES:2 * LANES] = v


def nsa_prep(proj, cos, sin, qw, kw, bd):
    t = proj.shape[0]
    tm = _pick(t, (640, 512, 384, 256, 128))
    return pl.pallas_call(
        _prep_body,
        grid=(t // tm,),
        in_specs=[pl.BlockSpec((tm, Q_DIM), lambda i: (i, Q0 // Q_DIM)),
                  pl.BlockSpec((tm, 6 * KV_DIM), lambda i: (i, KV0 // (6 * KV_DIM))),
                  pl.BlockSpec((tm, LANES), lambda i: (i, 0)),
                  pl.BlockSpec((tm, LANES), lambda i: (i, 0)),
                  pl.BlockSpec((1, LANES), lambda i: (0, 0)),
                  pl.BlockSpec((8, LANES), lambda i: (0, 0)),
                  pl.BlockSpec((LANES, LANES), lambda i: (0, 0))],
        out_specs=[pl.BlockSpec((tm, Q_DIM), lambda i: (i, 0)),
                   pl.BlockSpec((tm, 4 * KV_DIM), lambda i: (i, 0)),
                   pl.BlockSpec((tm, 2 * KV_DIM), lambda i: (i, 0))],
        out_shape=[jax.ShapeDtypeStruct((t, Q_DIM), BF16),
                   jax.ShapeDtypeStruct((t, 4 * KV_DIM), F32),
                   jax.ShapeDtypeStruct((t, 2 * KV_DIM), F32)],
        compiler_params=_cparams(("parallel",)),
        name="nsa_prep",
    )(proj, proj, cos, sin, qw, kw, bd)


def _cmp_body(xk_ref, xv_ref, wk_ref, wv_ref, kc_ref, vct_ref, *, ns):
    ak = jnp.zeros((ns, 2 * LANES), F32)
    av = jnp.zeros((ns, 2 * LANES), F32)
    for j in range(CMP_STRIDE):
        rk = xk_ref[0, pl.ds(j, ns, stride=CMP_STRIDE), :]
        rv = xv_ref[0, pl.ds(j, ns, stride=CMP_STRIDE), :]
        ak = ak + jnp.dot(rk.astype(BF16), wk_ref[j], preferred_element_type=F32)
        av = av + jnp.dot(rv.astype(BF16), wv_ref[j], preferred_element_type=F32)
    kc = ak[:, :LANES] + pltpu.roll(ak[:, LANES:], ns - 1, 0)
    vc = av[:, :LANES] + pltpu.roll(av[:, LANES:], ns - 1, 0)
    kc_ref[0] = kc.astype(BF16)
    vct_ref[0] = vc.T.astype(BF16)


def compress_rows(x3, col_block, wk, wv):
    b, r, _ = x3.shape
    ns = r // CMP_STRIDE
    return pl.pallas_call(
        functools.partial(_cmp_body, ns=ns),
        grid=(b,),
        in_specs=[pl.BlockSpec((1, r, LANES), lambda i: (i, 0, 2 * col_block)),
                  pl.BlockSpec((1, r, LANES), lambda i: (i, 0, 2 * col_block + 1)),
                  pl.BlockSpec((CMP_STRIDE, LANES, 2 * LANES), lambda i: (0, 0, 0)),
                  pl.BlockSpec((CMP_STRIDE, LANES, 2 * LANES), lambda i: (0, 0, 0))],
        out_specs=[pl.BlockSpec((1, ns, LANES), lambda i: (i, 0, 0)),
                   pl.BlockSpec((1, LANES, ns), lambda i: (i, 0, 0))],
        out_shape=[jax.ShapeDtypeStruct((b, ns, LANES), BF16),
                   jax.ShapeDtypeStruct((b, LANES, ns), BF16)],
        compiler_params=_cparams(("parallel",)),
        name="compress_rows",
    )(x3, x3, wk, wv)


def _relayout_body(x_ref, k_ref, vt_ref, *, nchunk):
    for c in range(nchunk):
        x = x_ref[0, c * CHUNK:(c + 1) * CHUNK, :]
        k_ref[0, c * CHUNK:(c + 1) * CHUNK, :] = x[:, :LANES].astype(BF16)
        vt_ref[0, c] = x[:, LANES:].T.astype(BF16)


def kv_relayout(x3, col_block):
    b, r, _ = x3.shape
    rt = _pick(r, (1024, 512, 256, 128))
    nchunk = rt // CHUNK
    return pl.pallas_call(
        functools.partial(_relayout_body, nchunk=nchunk),
        grid=(b, r // rt),
        in_specs=[pl.BlockSpec((1, rt, 2 * LANES), lambda i, j: (i, j, col_block))],
        out_specs=[pl.BlockSpec((1, rt, LANES), lambda i, j: (i, j, 0)),
                   pl.BlockSpec((1, nchunk, LANES, LANES), lambda i, j: (i, j, 0, 0))],
        out_shape=[jax.ShapeDtypeStruct((b, r, LANES), BF16),
                   jax.ShapeDtypeStruct((b, r // CHUNK, LANES, LANES), BF16)],
        compiler_params=_cparams(("parallel", "parallel")),
        name="kv_relayout",
    )(x3)


def _attn_body(*refs, nc, nbp, k_sel, pos0, has_extra, nkc, nwc):
    if has_extra:
        (q_ref, g_ref, kc_ref, vct_ref, st_ref, ks_ref, vst_ref, kw_ref, vwt_ref,
         ksx_ref, vstx_ref, kwx_ref, vwtx_ref, o_ref, gt_ref, sc_ref, sel_ref) = refs
    else:
        (q_ref, g_ref, kc_ref, vct_ref, st_ref, ks_ref, vst_ref, kw_ref, vwt_ref,
         o_ref, gt_ref, sc_ref, sel_ref) = refs
    i = pl.program_id(1)
    lane = lax.broadcasted_iota(jnp.int32, (1, LANES), 1)
    qpos = pos0 + i * CHUNK + lane
    row = lax.broadcasted_iota(jnp.int32, (CHUNK, 1), 0)
    cur_chunk = pos0 // CHUNK + i
    qt = q_ref[0].astype(F32).T.astype(BF16)
    gt_ref[...] = jax.nn.sigmoid(g_ref[0]).T
    half = HEAD_DIM

    def softmax_update(carry, k, vt, qh, mask):
        m, l, acc = carry
        s = jnp.dot(k, qh, preferred_element_type=F32)
        s = jnp.where(mask, s, NEG_INF)
        m_new = jnp.maximum(m, jnp.max(s, axis=0, keepdims=True))
        alpha = jnp.exp(m - m_new)
        p = jnp.where(mask, jnp.exp(s - m_new), 0.0)
        l = alpha * l + jnp.sum(p, axis=0, keepdims=True)
        acc = alpha * acc + jnp.dot(vt, p.astype(BF16), preferred_element_type=F32)
        return m_new, l, acc

    def finish(carry):
        _, l, acc = carry
        return acc * (1.0 / jnp.maximum(l, 1e-30))

    init = (jnp.full((1, LANES), NEG_INF, F32), jnp.zeros((1, LANES), F32), jnp.zeros((half, LANES), F32))
    blk = lax.broadcasted_iota(jnp.int32, (nbp, 1), 0)
    cur_blk = qpos // SEL_BLOCK
    valid = blk <= cur_blk
    forced = (blk == 0) | (blk == cur_blk) | (blk == cur_blk - 1)
    cend = lax.broadcasted_iota(jnp.int32, (nc, 1), 0) * CMP_STRIDE + (CMP_LEN - 1)
    mask_c = cend <= qpos

    for g in range(N_KV):
        gsl = slice(g * half, (g + 1) * half)
        zq = jnp.zeros((half, LANES), BF16)
        qh = []
        for r in range(GROUP):
            qr = qt[(GROUP * g + r) * half:(GROUP * g + r + 1) * half, :]
            qh.append(jnp.concatenate([qr, zq] if g == 0 else [zq, qr], axis=0))
        kc = kc_ref[0]
        vct = vct_ref[0, gsl, :]
        o_c = []
        psum = jnp.zeros((nc, LANES), F32)
        for r in range(GROUP):
            s = jnp.dot(kc, qh[r], preferred_element_type=F32)
            s = jnp.where(mask_c, s, NEG_INF)
            m = jnp.max(s, axis=0, keepdims=True)
            p = jnp.where(mask_c, jnp.exp(s - m), 0.0)
            l = jnp.sum(p, axis=0, keepdims=True)
            p = p * (1.0 / jnp.maximum(l, 1e-30))
            psum = psum + p
            o_c.append(jnp.dot(vct, p.astype(BF16), preferred_element_type=F32))
        imp = jnp.dot(st_ref[...], psum, precision=HIGHEST, preferred_element_type=F32)
        score = jnp.where(valid, jnp.where(forced, FORCED_SCORE, imp), -1.0)
        sc_ref[...] = score

        def rank_step(j, rank):
            rj = sc_ref[pl.ds(j, 1), :]
            ahead = (rj > score) | ((rj == score) & (j < blk))
            return rank + jnp.where(ahead, 1.0, 0.0)

        rank = lax.fori_loop(0, nbp, rank_step, jnp.zeros((nbp, LANES), F32))
        sel_ref[...] = jnp.where(rank < k_sel, 1.0, 0.0)

        def sel_mask(c):
            s0 = sel_ref[pl.ds(2 * c, 1), :]
            s1 = sel_ref[pl.ds(2 * c + 1, 1), :]
            chosen = jnp.where(row < SEL_BLOCK, s0, s1) > 0.5
            kpos = c * CHUNK + row
            return chosen & (kpos <= qpos)

        def sel_step(c, carry):
            k = ks_ref[0, pl.ds(pl.multiple_of(c * CHUNK, CHUNK), CHUNK), :]
            vt = vst_ref[0, c, gsl, :]
            mask = sel_mask(c)
            return tuple(softmax_update(carry[r], k, vt, qh[r], mask) for r in range(GROUP))

        n_loop = nkc if has_extra else i + 1
        carry = lax.fori_loop(0, n_loop, sel_step, tuple(init for _ in range(GROUP)))
        if has_extra:
            mask = sel_mask(nkc)
            k = ksx_ref[0]
            vt = vstx_ref[0, 0, gsl, :]
            carry = tuple(softmax_update(carry[r], k, vt, qh[r], mask) for r in range(GROUP))
        o_s = [finish(carry[r]) for r in range(GROUP)]

        def win_mask(c):
            dpos = qpos - (c * CHUNK + row)
            return (dpos >= 0) & (dpos < WINDOW)

        if has_extra:
            def win_step(lc, carry):
                c = cur_chunk - nwc + lc
                k = kw_ref[0, pl.ds(pl.multiple_of(lc * CHUNK, CHUNK), CHUNK), :]
                vt = vwt_ref[0, lc, gsl, :]
                mask = win_mask(c)
                return tuple(softmax_update(carry[r], k, vt, qh[r], mask) for r in range(GROUP))

            carry = lax.fori_loop(0, nwc, win_step, tuple(init for _ in range(GROUP)))
            mask = win_mask(cur_chunk)
            k = kwx_ref[0]
            vt = vwtx_ref[0, 0, gsl, :]
            carry = tuple(softmax_update(carry[r], k, vt, qh[r], mask) for r in range(GROUP))
        else:
            def win_step(c, carry):
                k = kw_ref[0, pl.ds(pl.multiple_of(c * CHUNK, CHUNK), CHUNK), :]
                vt = vwt_ref[0, c, gsl, :]
                mask = win_mask(c)
                return tuple(softmax_update(carry[r], k, vt, qh[r], mask) for r in range(GROUP))

            first = jnp.maximum(i - WINDOW // CHUNK, 0)
            carry = lax.fori_loop(first, i + 1, win_step, tuple(init for _ in range(GROUP)))
        o_w = [finish(carry[r]) for r in range(GROUP)]

        for r in range(GROUP):
            h = GROUP * g + r
            o = (gt_ref[pl.ds(3 * h, 1), :] * o_c[r] + gt_ref[pl.ds(3 * h + 1, 1), :] * o_s[r]
                 + gt_ref[pl.ds(3 * h + 2, 1), :] * o_w[r])
            if r % 2 == 0:
                pending = o
            else:
                pair = jnp.concatenate([pending, o], axis=0).T
                o_ref[0, :, (h // 2) * LANES:(h // 2 + 1) * LANES] = pair.astype(BF16)


def nsa_attention(q3, g3, kc, vct, st, ks, vst, kw, vwt, extra, *, pos0, n_blk):
    b, tq, _ = q3.shape
    nq = tq // CHUNK
    nc = kc.shape[1]
    nbp = st.shape[0]
    nkc = vst.shape[1]
    nwc = vwt.shape[1]
    has_extra = extra is not None
    per_b3 = lambda i, j: (i, 0, 0)
    per_b4 = lambda i, j: (i, 0, 0, 0)
    in_specs = [pl.BlockSpec((1, CHUNK, Q_DIM), lambda i, j: (i, j, 0)),
                pl.BlockSpec((1, CHUNK, LANES), lambda i, j: (i, j, 0)),
                pl.BlockSpec((1, nc, LANES), per_b3),
                pl.BlockSpec((1, LANES, nc), per_b3),
                pl.BlockSpec((nbp, nc), lambda i, j: (0, 0)),
                pl.BlockSpec((1, nkc * CHUNK, LANES), per_b3),
                pl.BlockSpec((1, nkc, LANES, LANES), per_b4),
                pl.BlockSpec((1, nwc * CHUNK, LANES), per_b3),
                pl.BlockSpec((1, nwc, LANES, LANES), per_b4)]
    args = [q3, g3, kc, vct, st, ks, vst, kw, vwt]
    if has_extra:
        in_specs += [pl.BlockSpec((1, CHUNK, LANES), per_b3), pl.BlockSpec((1, 1, LANES, LANES), per_b4),
                     pl.BlockSpec((1, CHUNK, LANES), per_b3), pl.BlockSpec((1, 1, LANES, LANES), per_b4)]
        args += list(extra)
    body = functools.partial(_attn_body, nc=nc, nbp=nbp, k_sel=min(N_SEL, n_blk), pos0=pos0,
                             has_extra=has_extra, nkc=nkc, nwc=nwc)
    return pl.pallas_call(
        body,
        grid=(b, nq),
        in_specs=in_specs,
        out_specs=pl.BlockSpec((1, CHUNK, Q_DIM), lambda i, j: (i, j, 0)),
        out_shape=jax.ShapeDtypeStruct((b, tq, Q_DIM), BF16),
        scratch_shapes=[pltpu.VMEM((LANES, LANES), F32), pltpu.VMEM((nbp, LANES), F32), pltpu.VMEM((nbp, LANES), F32)],
        compiler_params=_cparams(("parallel", "arbitrary")),
        name="nsa_attention",
    )(*args)


def _bc_body(xbc_ref, z_ref, sm_ref, scb_ref, scc_ref, sch_ref, h0_ref, tssm_ref, tsc_ref,
             cw_ref, cb_ref, dtb_ref, a_ref, dw_ref, nw_ref, scw_ref, ex_ref, ext_ref, tri_ref, trit_ref,
             ossm_ref, osc_ref, ht_ref, xp_ref, cp_ref, h_ref, *, valid_len):
    c = pl.program_id(1)
    nchunks = pl.num_programs(1)
    L = CHUNK

    @pl.when(c == 0)
    def _():
        xp_ref[0:8, :] = tssm_ref[0]
        cp_ref[0:8, :] = tsc_ref[0]
        h_ref[...] = h0_ref[0]

    x = xbc_ref[...]
    xp_ref[8:8 + L, :] = x
    conv = cb_ref[...] + cw_ref[0:1, :] * xp_ref[pl.ds(8 - 3, L), :]
    for k in range(1, SSM_CONV):
        conv = conv + cw_ref[k:k + 1, :] * xp_ref[pl.ds(8 - 3 + k, L), :]
    xp_ref[0:8, :] = x[L - 8:L, :]
    xbc = _silu(conv)
    xs = xbc[:, :SSM_INNER]
    bm = xbc[:, SSM_INNER:SSM_INNER + LANES]
    cm = xbc[:, SSM_INNER + LANES:]

    ch = scc_ref[...] * sch_ref[...]
    cp_ref[8:8 + L, :] = ch
    sc = scw_ref[0:1, :] * cp_ref[pl.ds(8 - 2, L), :]
    for k in range(1, SC_WIDTH):
        sc = sc + scw_ref[k:k + 1, :] * cp_ref[pl.ds(8 - 2 + k, L), :]
    cp_ref[0:8, :] = ch[L - 8:L, :]
    osc_ref[...] = (scb_ref[...] * sc).astype(osc_ref.dtype)

    raw = sm_ref[...] + dtb_ref[...]
    dt = jnp.maximum(raw, 0.0) + jnp.log1p(jnp.exp(-jnp.abs(raw)))
    if valid_len < L:
        trow = lax.broadcasted_iota(jnp.int32, (L, 1), 0)
        dt = jnp.where(trow < valid_len, dt, 0.0)
    da = dt * a_ref[...]
    acum = jnp.dot(tri_ref[...], da, precision=HIGHEST, preferred_element_type=F32)
    acum_t = jnp.dot(da.T, trit_ref[...], precision=HIGHEST, preferred_element_type=F32)
    ex = ex_ref[...]
    dt_w = jnp.dot(dt, ex, precision=HIGHEST, preferred_element_type=F32)
    acum_w = jnp.dot(acum, ex, precision=HIGHEST, preferred_element_type=F32)
    last_w = acum_w[L - 1:L, :]
    xdt = xs * dt_w
    xdt_b = xdt.astype(BF16)
    xtail_t = (xdt * jnp.exp(last_w - acum_w)).T.astype(BF16)
    eacum_w = jnp.exp(acum_w)
    acum_wt = jnp.dot(ext_ref[...], acum_t, precision=HIGHEST, preferred_element_type=F32)
    hdecay = jnp.exp(acum_wt[:, L - 1:L])

    lane = lax.broadcasted_iota(jnp.int32, (1, LANES), 1)
    trow2 = lax.broadcasted_iota(jnp.int32, (L, L), 0)
    tcol2 = lax.broadcasted_iota(jnp.int32, (L, L), 1)
    causal = tcol2 <= trow2
    ys = []
    heads_per_group = SSM_HEADS // SSM_GROUPS
    for g in range(SSM_GROUPS):
        gmask = (lane // SSM_STATE) == g
        cm_g = jnp.where(gmask, cm, 0.0).astype(BF16)
        bm_g = jnp.where(gmask, bm, 0.0).astype(BF16)
        cb = lax.dot_general(cm_g, bm_g, (((1,), (1,)), ((), ())), preferred_element_type=F32)
        for pair in range(heads_per_group // 2):
            pcol = g * (heads_per_group // 2) + pair
            psl = slice(pcol * LANES, (pcol + 1) * LANES)
            xpair = xdt_b[:, psl]
            y = jnp.zeros((L, LANES), F32)
            for e in range(2):
                h = 2 * pcol + e
                ac = acum[:, DT_LANE0 + h:DT_LANE0 + h + 1]
                ar = acum_t[DT_LANE0 + h:DT_LANE0 + h + 1, :]
                decay = jnp.exp(jnp.where(causal, ac - ar, NEG_INF))
                emask = (lane // SSM_HEAD_DIM) == e
                xh = jnp.where(emask, xpair, jnp.zeros_like(xpair))
                y = y + jnp.dot((cb * decay).astype(BF16), xh, preferred_element_type=F32)
            hp = h_ref[pcol * LANES:(pcol + 1) * LANES, :]
            y_off = lax.dot_general(cm_g, hp.astype(BF16), (((1,), (1,)), ((), ())), preferred_element_type=F32)
            ys.append(y + y_off * eacum_w[:, psl])
        rsl = slice(g * heads_per_group * SSM_HEAD_DIM, (g + 1) * heads_per_group * SSM_HEAD_DIM)
        upd = jnp.dot(xtail_t[rsl, :], bm_g, preferred_element_type=F32)
        h_ref[rsl, :] = h_ref[rsl, :] * hdecay[rsl, :] + upd

    y = jnp.concatenate(ys, axis=1)
    y = (y + xs * dw_ref[...]) * _silu(z_ref[...])
    ossm_ref[...] = _rms(y, nw_ref[...]).astype(ossm_ref.dtype)

    @pl.when(c == nchunks - 1)
    def _():
        ht_ref[0] = h_ref[...]


def ssd_shortconv(srcs, row_block0, nb, nchunks, h0, tail_ssm, tail_sc, consts, valid_len):
    cw, cb, dtb, a_full, dw, nw, scw, ex, ext, tri, trit = consts
    widths = (SSM_CONV_DIM, SSM_INNER, LANES, SC_DIM, SC_DIM, SC_DIM)
    in_specs, args = [], []
    for (arr, cblk), w in zip(srcs, widths):
        in_specs.append(pl.BlockSpec((CHUNK, w), functools.partial(
            lambda b, c, cblk: (row_block0 + b * nchunks + c, cblk), cblk=cblk)))
        args.append(arr)
    hp = SSM_HEADS * SSM_HEAD_DIM
    in_specs += [pl.BlockSpec((1, hp, LANES), lambda b, c: (b, 0, 0)),
                 pl.BlockSpec((1, 8, SSM_CONV_DIM), lambda b, c: (b, 0, 0)),
                 pl.BlockSpec((1, 8, SC_DIM), lambda b, c: (b, 0, 0))]
    args += [h0, tail_ssm, tail_sc]
    for cst in (cw, cb, dtb, a_full, dw, nw, scw, ex, ext, tri, trit):
        in_specs.append(pl.BlockSpec(cst.shape, lambda b, c: (0, 0)))
        args.append(cst)
    rows = nb * nchunks * CHUNK
    return pl.pallas_call(
        functools.partial(_bc_body, valid_len=valid_len),
        grid=(nb, nchunks),
        in_specs=in_specs,
        out_specs=[pl.BlockSpec((CHUNK, SSM_INNER), lambda b, c: (b * nchunks + c, 0)),
                   pl.BlockSpec((CHUNK, SC_DIM), lambda b, c: (b * nchunks + c, 0)),
                   pl.BlockSpec((1, hp, LANES), lambda b, c: (b, 0, 0))],
        out_shape=[jax.ShapeDtypeStruct((rows, SSM_INNER), BF16),
                   jax.ShapeDtypeStruct((rows, SC_DIM), BF16),
                   jax.ShapeDtypeStruct((nb, hp, LANES), F32)],
        scratch_shapes=[pltpu.VMEM((8 + CHUNK, SSM_CONV_DIM), F32), pltpu.VMEM((8 + CHUNK, SC_DIM), F32),
                        pltpu.VMEM((hp, LANES), F32)],
        compiler_params=_cparams(("parallel", "arbitrary")),
        name="ssd_shortconv",
    )(*args)


def _merge_body(on_ref, os_ref, oc_ref, g0_ref, g1_ref, g2_ref, x_ref, wb_ref, wo_ref, o_ref):
    y = jax.nn.sigmoid(g0_ref[...]) * jnp.dot(on_ref[...], wb_ref[0], preferred_element_type=F32)
    y = y + jax.nn.sigmoid(g1_ref[...]) * jnp.dot(os_ref[...], wb_ref[1], preferred_element_type=F32)
    y = y + jax.nn.sigmoid(g2_ref[...]) * jnp.dot(oc_ref[...], wb_ref[2], preferred_element_type=F32)
    o_ref[...] = x_ref[...] + jnp.dot(y.astype(BF16), wo_ref[...], preferred_element_type=F32)


def merge_out(o_nsa, o_ssm, o_sc, proj, x, wb, wo):
    t, d = x.shape
    tm = _pick(t, (640, 512, 384, 256, 128))
    row = lambda i: (i, 0)
    mg = MG0 // D_MODEL
    return pl.pallas_call(
        _merge_body,
        grid=(t // tm,),
        in_specs=[pl.BlockSpec((tm, BRANCH_DIM), row), pl.BlockSpec((tm, BRANCH_DIM), row),
                  pl.BlockSpec((tm, BRANCH_DIM), row),
                  pl.BlockSpec((tm, d), lambda i: (i, mg)), pl.BlockSpec((tm, d), lambda i: (i, mg + 1)),
                  pl.BlockSpec((tm, d), lambda i: (i, mg + 2)),
                  pl.BlockSpec((tm, d), row),
                  pl.BlockSpec((N_BRANCH, BRANCH_DIM, d), lambda i: (0, 0, 0)),
                  pl.BlockSpec((d, d), lambda i: (0, 0))],
        out_specs=pl.BlockSpec((tm, d), row),
        out_shape=jax.ShapeDtypeStruct((t, d), F32),
        compiler_params=_cparams(("parallel",)),
        name="merge_out",
    )(o_nsa, o_ssm, o_sc, proj, proj, proj, x, wb, wo)


def _mlp_body(te_ref, x_ref, nw_ref, wg_ref, wu_ref, wd_ref, sc_ref, o_ref, xn_ref, acc_ref, *, dense):
    f = pl.program_id(1)

    @pl.when(f == 0)
    def _():
        if dense:
            xn_ref[...] = _rms(x_ref[...], nw_ref[...]).astype(BF16)
        else:
            xn_ref[...] = x_ref[...]
        acc_ref[...] = jnp.zeros_like(acc_ref)

    xn = xn_ref[...]
    gate = jnp.dot(xn, wg_ref[0], preferred_element_type=F32)
    up = jnp.dot(xn, wu_ref[0], preferred_element_type=F32)
    act = (_silu(gate) * up).astype(BF16)
    acc_ref[...] += jnp.dot(act, wd_ref[0], preferred_element_type=F32)

    @pl.when(f == pl.num_programs(1) - 1)
    def _():
        if dense:
            o_ref[...] = x_ref[...] + acc_ref[...]
        else:
            o_ref[...] = sc_ref[...] * acc_ref[...]


def swiglu_mlp(x, nw, wg, wu, wd, tile_expert, row_scale, *, dense, tm):
    r, d = x.shape
    ff = wg.shape[2]
    tf = _pick(ff, (1408, 896, 512, 256, 128))
    grid_spec = pltpu.PrefetchScalarGridSpec(
        num_scalar_prefetch=1,
        grid=(r // tm, ff // tf),
        in_specs=[pl.BlockSpec((tm, d), lambda i, f, te: (i, 0)),
                  pl.BlockSpec((1, d), lambda i, f, te: (0, 0)),
                  pl.BlockSpec((1, d, tf), lambda i, f, te: (te[i], 0, f)),
                  pl.BlockSpec((1, d, tf), lambda i, f, te: (te[i], 0, f)),
                  pl.BlockSpec((1, tf, d), lambda i, f, te: (te[i], f, 0)),
                  pl.BlockSpec((tm, 1), lambda i, f, te: (i, 0))],
        out_specs=pl.BlockSpec((tm, d), lambda i, f, te: (i, 0)),
        scratch_shapes=[pltpu.VMEM((tm, d), BF16), pltpu.VMEM((tm, d), F32)],
    )
    return pl.pallas_call(
        functools.partial(_mlp_body, dense=dense),
        grid_spec=grid_spec,
        out_shape=jax.ShapeDtypeStruct((r, d), F32),
        compiler_params=_cparams(("parallel", "arbitrary")),
        name="swiglu_dense" if dense else "swiglu_grouped",
    )(tile_expert, x, nw.reshape(1, d), wg, wu, wd, row_scale)


def _router_body(x_ref, nw_ref, wr_ref, hn_ref, lg_ref):
    hn = _rms(x_ref[...], nw_ref[...])
    hn_ref[...] = hn.astype(BF16)
    lg_ref[...] = jnp.dot(hn, wr_ref[...], precision=HIGHEST, preferred_element_type=F32)


def moe_router(x, nw, wr_pad):
    t, d = x.shape
    tm = _pick(t, (640, 512, 384, 256, 128))
    return pl.pallas_call(
        _router_body,
        grid=(t // tm,),
        in_specs=[pl.BlockSpec((tm, d), lambda i: (i, 0)), pl.BlockSpec((1, d), lambda i: (0, 0)),
                  pl.BlockSpec((d, LANES), lambda i: (0, 0))],
        out_specs=[pl.BlockSpec((tm, d), lambda i: (i, 0)), pl.BlockSpec((tm, LANES), lambda i: (i, 0))],
        out_shape=[jax.ShapeDtypeStruct((t, d), BF16), jax.ShapeDtypeStruct((t, LANES), F32)],
        compiler_params=_cparams(("parallel",)),
        name="moe_router",
    )(x, nw.reshape(1, d), wr_pad)


def moe_ffn(x, nw, router, wg, wu, wd, tm):
    t, d = x.shape
    ne = router.shape[1]
    hn, logits = moe_router(x, nw, jnp.pad(router, ((0, 0), (0, LANES - ne))))
    top_v, top_i = lax.top_k(logits[:, :ne], TOP_K)
    gate = jax.nn.softmax(top_v, axis=-1)
    npair = t * TOP_K
    e_flat = top_i.reshape(npair)
    onehot = (e_flat[:, None] == jnp.arange(ne)[None, :]).astype(jnp.int32)
    csum = jnp.cumsum(onehot, axis=0)
    counts = csum[-1]
    rank = jnp.take_along_axis(csum, e_flat[:, None], axis=1)[:, 0] - 1
    tiles_e = (counts + tm - 1) // tm
    tile_end = jnp.cumsum(tiles_e)
    tile_start = tile_end - tiles_e
    n_tiles = npair // tm + ne
    dest = tile_start[e_flat] * tm + rank
    tile_ids = jnp.arange(n_tiles)
    tile_expert = jnp.minimum(jnp.sum(tile_ids[:, None] >= tile_end[None, :], axis=1), ne - 1).astype(jnp.int32)
    order = jnp.argsort(e_flat, stable=True)
    cnt_start = jnp.cumsum(counts) - counts
    rows = jnp.arange(n_tiles * tm)
    row_e = jnp.repeat(tile_expert, tm)
    within = rows - jnp.repeat(tile_start[tile_expert], tm) * tm
    row_valid = (within < counts[row_e]) & (jnp.repeat(tile_ids, tm) < tile_end[ne - 1])
    src_pair = order[jnp.clip(cnt_start[row_e] + within, 0, npair - 1)]
    xg = jnp.take(hn, src_pair // TOP_K, axis=0)
    scale = jnp.where(row_valid, gate.reshape(npair)[src_pair], 0.0).astype(F32)[:, None]
    y = swiglu_mlp(xg, nw, wg, wu, wd, tile_expert, scale, dense=False, tm=tm)
    contrib = jnp.take(y, dest, axis=0).reshape(t, TOP_K, d)
    return x + contrib[:, 0] + contrib[:, 1]


def _perm_w_in(w):
    d = w.shape[0]
    seg = lambda o, n: w[:, o:o + n]
    small = jnp.concatenate([seg(_OG, 3 * N_HEADS), seg(_ODT, SSM_HEADS),
                             jnp.zeros((d, LANES - 3 * N_HEADS - SSM_HEADS), w.dtype)], axis=1)
    out = jnp.concatenate([seg(_OQ, Q_DIM), seg(_OZ, SSM_INNER), seg(_OSCB, SC_DIM), seg(_OKV, 6 * KV_DIM),
                           seg(_OXBC, SSM_CONV_DIM), seg(_OSCC, SC_DIM), seg(_OSCH, SC_DIM),
                           seg(_OMG, N_BRANCH * D_MODEL), small,
                           jnp.zeros((d, N_PROJ - SM0 - LANES), w.dtype)], axis=1)
    return out.astype(BF16)


def _cmp_weights(w_cmp_l):
    wh = w_cmp_l.reshape(2, 2, CMP_STRIDE, HEAD_DIM, HEAD_DIM)
    eye = jnp.eye(N_KV, dtype=w_cmp_l.dtype)
    big = jnp.einsum('ahjde,gG->ajgdhGe', wh, eye)
    big = big.reshape(2, CMP_STRIDE, N_KV * HEAD_DIM, 2 * N_KV * HEAD_DIM).astype(BF16)
    return big[0], big[1]


def _rope_tables(pos):
    halfd = HEAD_DIM // 2
    inv_freq = jnp.exp(-math.log(ROPE_THETA) * jnp.arange(halfd, dtype=F32) / halfd)
    ang = pos.astype(F32)[:, None] * inv_freq[None, :]
    cos, sin = jnp.cos(ang), jnp.sin(ang)
    cos_t = jnp.concatenate([cos, cos, cos, cos], axis=1)
    sin_t = jnp.concatenate([-sin, sin, -sin, sin], axis=1)
    return cos_t, sin_t


def _block_sum_matrix(n_blk, nbp, nc):
    ratio = SEL_BLOCK // CMP_STRIDE
    b = jnp.arange(nbp)[:, None]
    c = jnp.arange(nc)[None, :]
    return ((c // ratio == b) & (b < n_blk)).astype(F32)


def _pad_rows(x, n):
    return jnp.pad(x, ((0, 0), (0, n - x.shape[1])) + ((0, 0),) * (x.ndim - 2))


def kernel(x_prompt, x_sample, cache_nsa_pages, cache_nsa_window, state_ssm, state_ssm_conv, state_shortconv, page_table, norm_mix, w_in, q_norm, k_norm, w_cmp, ssm_conv_w, ssm_conv_b, ssm_dt_bias, ssm_a_log, ssm_d, ssm_norm, sc_conv_w, w_branch, w_out, norm_ffn, ffn_w_gate, ffn_w_up, ffn_w_down, moe_router, moe_w_gate, moe_w_up, moe_w_down):
    bp, tp, d = x_prompt.shape
    bd, td, _ = x_sample.shape
    depth = w_in.shape[0]
    page = cache_nsa_pages.shape[2]
    past_len = page_table.shape[1] * page
    n_tp, n_ts = bp * tp, bd * SAMPLE_SLOT
    lw = cache_nsa_window.shape[2]
    assert tp % CHUNK == 0 and past_len % CHUNK == 0 and lw % CHUNK == 0 and td <= SAMPLE_SLOT and lw == WINDOW
    assert td < CMP_STRIDE and past_len % CMP_STRIDE == 0
    nqp = tp // CHUNK

    xs_pad = _pad_rows(x_sample, SAMPLE_SLOT).reshape(n_ts, d)
    x = jnp.concatenate([x_prompt.reshape(n_tp, d), xs_pad], axis=0)

    pos_all = jnp.concatenate([jnp.tile(jnp.arange(tp, dtype=jnp.int32), bp),
                               jnp.tile(past_len + jnp.arange(SAMPLE_SLOT, dtype=jnp.int32), bd)])
    cos_t, sin_t = _rope_tables(pos_all)
    lane = jnp.arange(LANES)
    bd_mat = ((lane[:, None] // HEAD_DIM) == (lane[None, :] // HEAD_DIM)).astype(F32) / HEAD_DIM
    n_blk_p = -(-tp // SEL_BLOCK)
    nc_p = tp // CMP_STRIDE
    st_p = _block_sum_matrix(n_blk_p, -(-n_blk_p // 8) * 8, nc_p)
    n_blk_s = -(-(past_len + td) // SEL_BLOCK)
    nc_s = past_len // CMP_STRIDE
    st_s = _block_sum_matrix(n_blk_s, -(-(n_blk_s + 1) // 8) * 8, nc_s)
    hl = jnp.arange(SSM_HEADS)
    ex = jnp.zeros((LANES, SSM_INNER), F32).at[DT_LANE0 + jnp.repeat(hl, SSM_HEAD_DIM), jnp.arange(SSM_INNER)].set(1.0)
    tri = (jnp.arange(CHUNK)[None, :] <= jnp.arange(CHUNK)[:, None]).astype(F32)
    trit = tri.T
    zero_h = jnp.zeros((bp, SSM_HEADS * SSM_HEAD_DIM, LANES), F32)
    zero_tssm = jnp.zeros((bp, 8, SSM_CONV_DIM), F32)
    zero_tsc = jnp.zeros((bp, 8, SC_DIM), F32)
    heads_per_group = SSM_HEADS // SSM_GROUPS

    def lanes_24(v):
        return jnp.zeros((1, LANES), F32).at[0, DT_LANE0:DT_LANE0 + SSM_HEADS].set(v.astype(F32))

    def state_to_kernel(h):
        b = h.shape[0]
        hg = h.reshape(b, SSM_GROUPS, heads_per_group * SSM_HEAD_DIM, SSM_STATE)
        out = jnp.zeros((b, SSM_GROUPS, heads_per_group * SSM_HEAD_DIM, SSM_GROUPS, SSM_STATE), F32)
        for g in range(SSM_GROUPS):
            out = out.at[:, g, :, g, :].set(hg[:, g])
        return out.reshape(b, SSM_HEADS * SSM_HEAD_DIM, LANES)

    def state_from_kernel(hk):
        b = hk.shape[0]
        h5 = hk.reshape(b, SSM_GROUPS, heads_per_group * SSM_HEAD_DIM, SSM_GROUPS, SSM_STATE)
        hg = jnp.stack([h5[:, g, :, g, :] for g in range(SSM_GROUPS)], axis=1)
        return hg.reshape(b, SSM_HEADS, SSM_HEAD_DIM, SSM_STATE)

    outs = {k: [] for k in ("rows_p", "rows_s", "win_p", "win_s", "ssm_p", "ssm_s", "sconv_p", "sconv_s",
                            "cconv_p", "cconv_s")}
    tm_moe = 512
    for l in range(depth):
        proj = norm_matmul(x, norm_mix[l], _perm_w_in(w_in[l]))
        qw = jnp.tile(q_norm[l], 2).reshape(1, LANES)
        kw = jnp.pad(jnp.tile(k_norm[l], (1, 2)), ((0, 5), (0, 0)))
        q_all, rows_all, win_all = nsa_prep(proj, cos_t, sin_t, qw, kw, bd_mat)
        wk, wv = _cmp_weights(w_cmp[l])

        rows_p3 = rows_all[:n_tp].reshape(bp, tp, 4 * KV_DIM)
        win_p3 = win_all[:n_tp].reshape(bp, tp, 2 * KV_DIM)
        kc_p, vct_p = compress_rows(rows_p3, 0, wk, wv)
        ks_p, vst_p = kv_relayout(rows_p3, 1)
        kw_p, vwt_p = kv_relayout(win_p3, 0)
        q_p3 = q_all[:n_tp].reshape(bp, tp, Q_DIM)
        g_p3 = proj[:n_tp, SM0:SM0 + LANES].reshape(bp, tp, LANES)
        o_nsa_p = nsa_attention(q_p3, g_p3, kc_p, vct_p, st_p, ks_p, vst_p, kw_p, vwt_p, None,
                                pos0=0, n_blk=n_blk_p)

        past = cache_nsa_pages[l][page_table].reshape(bd, past_len, 4 * KV_DIM)
        kc_s, vct_s = compress_rows(past, 0, wk, wv)
        ks_s, vst_s = kv_relayout(past, 1)
        kw_s, vwt_s = kv_relayout(cache_nsa_window[l].reshape(bd, lw, 2 * KV_DIM), 0)
        rows_s3 = rows_all[n_tp:].reshape(bd, SAMPLE_SLOT, 4 * KV_DIM)
        win_s3 = win_all[n_tp:].reshape(bd, SAMPLE_SLOT, 2 * KV_DIM)
        tmask = (jnp.arange(SAMPLE_SLOT) < td)[None, :, None]
        ksx, vstx = kv_relayout(_pad_rows(jnp.where(tmask, rows_s3, 0.0), CHUNK), 1)
        kwx, vwtx = kv_relayout(_pad_rows(jnp.where(tmask, win_s3, 0.0), CHUNK), 0)
        q_s3 = _pad_rows(q_all[n_tp:].reshape(bd, SAMPLE_SLOT, Q_DIM), CHUNK)
        g_s3 = _pad_rows(proj[n_tp:, SM0:SM0 + LANES].reshape(bd, SAMPLE_SLOT, LANES), CHUNK)
        o_nsa_s = nsa_attention(q_s3, g_s3, kc_s, vct_s, st_s, ks_s, vst_s, kw_s, vwt_s,
                                (ksx, vstx, kwx, vwtx), pos0=past_len, n_blk=n_blk_s)
        o_nsa = jnp.concatenate([o_nsa_p.reshape(n_tp, Q_DIM),
                                 o_nsa_s[:, :SAMPLE_SLOT].reshape(n_ts, Q_DIM)], axis=0)

        consts = (ssm_conv_w[l], ssm_conv_b[l].reshape(1, -1), lanes_24(ssm_dt_bias[l]),
                  lanes_24(-jnp.exp(ssm_a_log[l].astype(F32))),
                  jnp.repeat(ssm_d[l].astype(F32), SSM_HEAD_DIM).reshape(1, -1), ssm_norm[l].reshape(1, -1),
                  sc_conv_w[l], ex, ex.T, tri, trit)
        col_blocks = (XBC0 // SSM_CONV_DIM, Z0 // SSM_INNER, SM0 // LANES, SCB0 // SC_DIM, SCC0 // SC_DIM,
                      SCH0 // SC_DIM)
        o_ssm_p, o_sc_p, h_p = ssd_shortconv([(proj, cb) for cb in col_blocks], 0, bp, nqp,
                                             zero_h, zero_tssm, zero_tsc, consts, CHUNK)
        proj_s = proj[n_tp:].reshape(bd, SAMPLE_SLOT, N_PROJ)
        seg_s = lambda o, w: _pad_rows(proj_s[:, :, o:o + w], CHUNK).reshape(bd * CHUNK, w)
        srcs_s = [(seg_s(XBC0, SSM_CONV_DIM), 0), (seg_s(Z0, SSM_INNER), 0), (seg_s(SM0, LANES), 0),
                  (seg_s(SCB0, SC_DIM), 0), (seg_s(SCC0, SC_DIM), 0), (seg_s(SCH0, SC_DIM), 0)]
        tssm = jnp.pad(state_ssm_conv[l], ((0, 0), (8 - (SSM_CONV - 1), 0), (0, 0)))
        tsc = jnp.pad(state_shortconv[l], ((0, 0), (8 - (SC_WIDTH - 1), 0), (0, 0)))
        o_ssm_s, o_sc_s, h_s = ssd_shortconv(srcs_s, 0, bd, 1, state_to_kernel(state_ssm[l].astype(F32)),
                                             tssm, tsc, consts, td)
        take_s = lambda a: a.reshape(bd, CHUNK, -1)[:, :SAMPLE_SLOT].reshape(n_ts, -1)
        o_ssm = jnp.concatenate([o_ssm_p, take_s(o_ssm_s)], axis=0)
        o_sc = jnp.concatenate([o_sc_p, take_s(o_sc_s)], axis=0)

        x = merge_out(o_nsa, o_ssm, o_sc, proj, x, w_branch[l].astype(BF16), w_out[l].astype(BF16))

        if l % 2 == 0:
            i = l // 2
            tm = _pick(x.shape[0], (640, 512, 384, 256, 128))
            x = swiglu_mlp(x, norm_ffn[l], ffn_w_gate[i:i + 1].astype(BF16), ffn_w_up[i:i + 1].astype(BF16),
                           ffn_w_down[i:i + 1].astype(BF16), jnp.zeros((x.shape[0] // tm,), jnp.int32),
                           jnp.ones((x.shape[0], 1), F32), dense=True, tm=tm)
        else:
            i = l // 2
            x = moe_ffn(x, norm_ffn[l], moe_router[i], moe_w_gate[i].astype(BF16), moe_w_up[i].astype(BF16),
                        moe_w_down[i].astype(BF16), tm_moe)

        shp = (N_KV, HEAD_DIM)
        outs["rows_p"].append(rows_p3.reshape(bp, tp, 4, *shp))
        outs["rows_s"].append(rows_s3[:, :td].reshape(bd, td, 4, *shp))
        keep_p = min(WINDOW, tp)
        outs["win_p"].append(win_p3[:, tp - keep_p:].reshape(bp, keep_p, 2, *shp))
        win_cat = jnp.concatenate([cache_nsa_window[l].reshape(bd, lw, 2 * KV_DIM), win_s3[:, :td]], axis=1)
        outs["win_s"].append(win_cat[:, td:].reshape(bd, lw, 2, *shp))
        outs["ssm_p"].append(state_from_kernel(h_p))
        outs["ssm_s"].append(state_from_kernel(h_s))
        tail_p = proj[:n_tp].reshape(bp, tp, N_PROJ)[:, tp - (SSM_CONV - 1):]
        outs["sconv_p"].append(tail_p[:, :, XBC0:XBC0 + SSM_CONV_DIM])
        xbc_cat = jnp.concatenate([state_ssm_conv[l], proj_s[:, :td, XBC0:XBC0 + SSM_CONV_DIM]], axis=1)
        outs["sconv_s"].append(xbc_cat[:, td:])
        tail_c = tail_p[:, (SSM_CONV - 1) - (SC_WIDTH - 1):]
        outs["cconv_p"].append(tail_c[:, :, SCC0:SCC0 + SC_DIM] * tail_c[:, :, SCH0:SCH0 + SC_DIM])
        ch_s = proj_s[:, :td, SCC0:SCC0 + SC_DIM] * proj_s[:, :td, SCH0:SCH0 + SC_DIM]
        outs["cconv_s"].append(jnp.concatenate([state_shortconv[l], ch_s], axis=1)[:, td:])

    y_prompt = x[:n_tp].reshape(bp, tp, d)
    y_sample = x[n_tp:].reshape(bd, SAMPLE_SLOT, d)[:, :td]
    st = lambda k: jnp.stack(outs[k])
    return (y_prompt, y_sample, st("rows_p"), st("rows_s"), st("win_p"), st("win_s"), st("ssm_p"), st("ssm_s"),
            st("sconv_p"), st("sconv_s"), st("cconv_p"), st("cconv_s"))
```

```python
import functools
import math

import jax
import jax.numpy as jnp
from jax import lax
from jax.experimental import pallas as pl
from jax.experimental.pallas import tpu as pltpu

F32 = jnp.float32
BF16 = jnp.bfloat16
HIGHEST = lax.Precision.HIGHEST

D_MODEL = 1024
N_HEADS = 8
N_KV = 2
HEAD_DIM = 64
GROUP = N_HEADS // N_KV
CMP_LEN = 32
CMP_STRIDE = 16
SEL_BLOCK = 64
N_SEL = 16
WINDOW = 512
ROPE_THETA = 10000.0
SSM_HEADS = 8
SSM_HEAD_DIM = 64
SSM_INNER = 512
SSM_GROUPS = 2
SSM_STATE = 64
SSM_CONV = 4
SSM_CONV_DIM = 768
SC_DIM = 512
SC_WIDTH = 3
N_BRANCH = 3
BRANCH_DIM = 512
N_EXPERTS = 8
TOP_K = 2
EPS = 1e-6
NEG_INF = -1e30
FORCED_SCORE = 1e4
Q_DIM = 512
KV_DIM = 128

LANES = 128
CHUNK = 128
SAMPLE_SLOT = 8
VMEM_LIMIT = 56 * 1024 * 1024

Q0, Z0, SCB0, KV0, XBC0, SCC0, SCH0, MG0, SM0, N_PROJ = 0, 512, 1024, 1536, 2304, 3072, 3584, 4096, 7168, 7680
GATE_LANE0 = 0
DT_LANE0 = 24
_OQ, _OKV, _OG, _OZ, _OXBC, _ODT, _OSCB, _OSCC, _OSCH, _OMG = 0, 512, 1280, 1304, 1816, 2584, 2592, 3104, 3616, 4128


def _cparams(sem):
    return pltpu.CompilerParams(dimension_semantics=sem, vmem_limit_bytes=VMEM_LIMIT)


def _pick(n, cands):
    for c in cands:
        if n % c == 0:
            return c
    raise ValueError(f"no tile for {n}")


def _rms(x, w):
    ms = jnp.mean(x * x, axis=-1, keepdims=True)
    return (x * lax.rsqrt(ms + EPS)) * w


def _silu(x):
    return x * jax.nn.sigmoid(x)


def _norm_matmul_body(x_ref, g_ref, w_ref, o_ref, xn_ref):
    @pl.when(pl.program_id(1) == 0)
    def _():
        xn_ref[...] = _rms(x_ref[...], g_ref[...]).astype(BF16)

    o_ref[...] = jnp.dot(xn_ref[...], w_ref[...], preferred_element_type=F32)


def norm_matmul(x, g, w):
    t, d = x.shape
    n = w.shape[1]
    tm = _pick(t, (1280, 1024, 640, 512, 384, 256, 128))
    tn = _pick(n, (1536, 1280, 1024, 512, 256, 128))
    return pl.pallas_call(
        _norm_matmul_body,
        grid=(t // tm, n // tn),
        in_specs=[pl.BlockSpec((tm, d), lambda i, j: (i, 0)),
                  pl.BlockSpec((1, d), lambda i, j: (0, 0)),
                  pl.BlockSpec((d, tn), lambda i, j: (0, j))],
        out_specs=pl.BlockSpec((tm, tn), lambda i, j: (i, j)),
        out_shape=jax.ShapeDtypeStruct((t, n), F32),
        scratch_shapes=[pltpu.VMEM((tm, d), BF16)],
        compiler_params=_cparams(("parallel", "arbitrary")),
        name="norm_matmul",
    )(x, g.reshape(1, d), w)


def _prep_body(q_ref, kv_ref, cos_ref, sin_ref, qw_ref, kw_ref, bd_ref, qo_ref, rows_ref, win_ref):
    cos = cos_ref[...]
    sin = sin_ref[...]
    bd = bd_ref[...]
    lane = lax.broadcasted_iota(jnp.int32, cos.shape, 1)
    first_half = (lane % HEAD_DIM) < (HEAD_DIM // 2)

    def norm_rope(x, w):
        ms = jnp.dot(x * x, bd, precision=HIGHEST, preferred_element_type=F32)
        y = (x * lax.rsqrt(ms + EPS)) * w
        partner = jnp.where(first_half, pltpu.roll(y, LANES - HEAD_DIM // 2, 1), pltpu.roll(y, HEAD_DIM // 2, 1))
        return y * cos + partner * sin

    scale = HEAD_DIM ** -0.5
    for c in range(Q_DIM // LANES):
        sl = slice(c * LANES, (c + 1) * LANES)
        qo_ref[:, sl] = (norm_rope(q_ref[:, sl], qw_ref[...]) * scale).astype(BF16)
    for br in range(3):
        k = norm_rope(kv_ref[:, (2 * br) * LANES:(2 * br + 1) * LANES], kw_ref[br:br + 1, :])
        v = kv_ref[:, (2 * br + 1) * LANES:(2 * br + 2) * LANES]
        if br < 2:
            rows_ref[:, (2 * br) * LANES:(2 * br + 1) * LANES] = k
            rows_ref[:, (2 * br + 1) * LANES:(2 * br + 2) * LANES] = v
        else:
            win_ref[:, 0:LANES] = k
            win_ref[:, LANES:2 * LANES] = v


def nsa_prep(proj, cos, sin, qw, kw, bd):
    t = proj.shape[0]
    tm = _pick(t, (640, 512, 384, 256, 128))
    return pl.pallas_call(
        _prep_body,
        grid=(t // tm,),
        in_specs=[pl.BlockSpec((tm, Q_DIM), lambda i: (i, Q0 // Q_DIM)),
                  pl.BlockSpec((tm, 6 * KV_DIM), lambda i: (i, KV0 // (6 * KV_DIM))),
                  pl.BlockSpec((tm, LANES), lambda i: (i, 0)),
                  pl.BlockSpec((tm, LANES), lambda i: (i, 0)),
                  pl.BlockSpec((1, LANES), lambda i: (0, 0)),
                  pl.BlockSpec((8, LANES), lambda i: (0, 0)),
                  pl.BlockSpec((LANES, LANES), lambda i: (0, 0))],
        out_specs=[pl.BlockSpec((tm, Q_DIM), lambda i: (i, 0)),
                   pl.BlockSpec((tm, 4 * KV_DIM), lambda i: (i, 0)),
                   pl.BlockSpec((tm, 2 * KV_DIM), lambda i: (i, 0))],
        out_shape=[jax.ShapeDtypeStruct((t, Q_DIM), BF16),
                   jax.ShapeDtypeStruct((t, 4 * KV_DIM), F32),
                   jax.ShapeDtypeStruct((t, 2 * KV_DIM), F32)],
        compiler_params=_cparams(("parallel",)),
        name="nsa_prep",
    )(proj, proj, cos, sin, qw, kw, bd)


def _prep_prompt_body(q_ref, kv_ref, cos_ref, sin_ref, qw_ref, kw_ref, bd_ref,
                      qo_ref, cmp_ref, ks_ref, kwo_ref, vst_ref, vwt_ref, rowst_ref, wint_ref, *, tm):
    cos = cos_ref[...]
    sin = sin_ref[...]
    bd = bd_ref[...]
    lane = lax.broadcasted_iota(jnp.int32, cos.shape, 1)
    first_half = (lane % HEAD_DIM) < (HEAD_DIM // 2)

    def norm_rope(x, w):
        ms = jnp.dot(x * x, bd, precision=HIGHEST, preferred_element_type=F32)
        y = (x * lax.rsqrt(ms + EPS)) * w
        partner = jnp.where(first_half, pltpu.roll(y, LANES - HEAD_DIM // 2, 1), pltpu.roll(y, HEAD_DIM // 2, 1))
        return y * cos + partner * sin

    scale = HEAD_DIM ** -0.5
    for c in range(Q_DIM // LANES):
        sl = slice(c * LANES, (c + 1) * LANES)
        qo_ref[:, sl] = (norm_rope(q_ref[:, sl], qw_ref[...]) * scale).astype(BF16)
    for a in range(6):
        x = kv_ref[:, a * LANES:(a + 1) * LANES]
        if a % 2 == 0:
            x = norm_rope(x, kw_ref[a // 2:a // 2 + 1, :])
        xt = x.T
        if a < 4:
            rowst_ref[0, a * LANES:(a + 1) * LANES, :] = xt
        else:
            wint_ref[0, (a - 4) * LANES:(a - 3) * LANES, :] = xt
        if a < 2:
            cmp_ref[:, a * LANES:(a + 1) * LANES] = x
        elif a == 2:
            ks_ref[...] = x.astype(BF16)
        elif a == 4:
            kwo_ref[...] = x.astype(BF16)
        else:
            dst = vst_ref if a == 3 else vwt_ref
            for c in range(tm // CHUNK):
                dst[0, c] = xt[:, c * CHUNK:(c + 1) * CHUNK].astype(BF16)


def nsa_prep_prompt(proj, nb, tp, cos, sin, qw, kw, bd):
    tm = _pick(tp, (512, 256, 128))
    nj = tp // tm
    row = lambda b, j: (b * nj + j, 0)
    return pl.pallas_call(
        functools.partial(_prep_prompt_body, tm=tm),
        grid=(nb, nj),
        in_specs=[pl.BlockSpec((tm, Q_DIM), lambda b, j: (b * nj + j, Q0 // Q_DIM)),
                  pl.BlockSpec((tm, 6 * KV_DIM), lambda b, j: (b * nj + j, KV0 // (6 * KV_DIM))),
                  pl.BlockSpec((tm, LANES), lambda b, j: (j, 0)),
                  pl.BlockSpec((tm, LANES), lambda b, j: (j, 0)),
                  pl.BlockSpec((1, LANES), lambda b, j: (0, 0)),
                  pl.BlockSpec((8, LANES), lambda b, j: (0, 0)),
                  pl.BlockSpec((LANES, LANES), lambda b, j: (0, 0))],
        out_specs=[pl.BlockSpec((tm, Q_DIM), row),
                   pl.BlockSpec((tm, 2 * LANES), row),
                   pl.BlockSpec((tm, LANES), row),
                   pl.BlockSpec((tm, LANES), row),
                   pl.BlockSpec((1, tm // CHUNK, LANES, LANES), lambda b, j: (b, j, 0, 0)),
                   pl.BlockSpec((1, tm // CHUNK, LANES, LANES), lambda b, j: (b, j, 0, 0)),
                   pl.BlockSpec((1, 4 * KV_DIM, tm), lambda b, j: (b, 0, j)),
                   pl.BlockSpec((1, 2 * KV_DIM, tm), lambda b, j: (b, 0, j))],
        out_shape=[jax.ShapeDtypeStruct((nb * tp, Q_DIM), BF16),
                   jax.ShapeDtypeStruct((nb * tp, 2 * LANES), F32),
                   jax.ShapeDtypeStruct((nb * tp, LANES), BF16),
                   jax.ShapeDtypeStruct((nb * tp, LANES), BF16),
                   jax.ShapeDtypeStruct((nb, tp // CHUNK, LANES, LANES), BF16),
                   jax.ShapeDtypeStruct((nb, tp // CHUNK, LANES, LANES), BF16),
                   jax.ShapeDtypeStruct((nb, 4 * KV_DIM, tp), F32),
                   jax.ShapeDtypeStruct((nb, 2 * KV_DIM, tp), F32)],
        compiler_params=_cparams(("parallel", "parallel")),
        name="nsa_prep_prompt",
    )(proj, proj, cos, sin, qw, kw, bd)


def _cmp_body(xk_ref, xv_ref, wk_ref, wv_ref, kc_ref, vct_ref, *, ns):
    ak = jnp.zeros((ns, 2 * LANES), F32)
    av = jnp.zeros((ns, 2 * LANES), F32)
    for j in range(CMP_STRIDE):
        rk = xk_ref[0, pl.ds(j, ns, stride=CMP_STRIDE), :]
        rv = xv_ref[0, pl.ds(j, ns, stride=CMP_STRIDE), :]
        ak = ak + jnp.dot(rk.astype(BF16), wk_ref[j], preferred_element_type=F32)
        av = av + jnp.dot(rv.astype(BF16), wv_ref[j], preferred_element_type=F32)
    kc = ak[:, :LANES] + pltpu.roll(ak[:, LANES:], ns - 1, 0)
    vc = av[:, :LANES] + pltpu.roll(av[:, LANES:], ns - 1, 0)
    kc_ref[0] = kc.astype(BF16)
    vct_ref[0] = vc.T.astype(BF16)


def compress_rows(x3, col_block, wk, wv):
    b, r, _ = x3.shape
    ns = r // CMP_STRIDE
    return pl.pallas_call(
        functools.partial(_cmp_body, ns=ns),
        grid=(b,),
        in_specs=[pl.BlockSpec((1, r, LANES), lambda i: (i, 0, 2 * col_block)),
                  pl.BlockSpec((1, r, LANES), lambda i: (i, 0, 2 * col_block + 1)),
                  pl.BlockSpec((CMP_STRIDE, LANES, 2 * LANES), lambda i: (0, 0, 0)),
                  pl.BlockSpec((CMP_STRIDE, LANES, 2 * LANES), lambda i: (0, 0, 0))],
        out_specs=[pl.BlockSpec((1, ns, LANES), lambda i: (i, 0, 0)),
                   pl.BlockSpec((1, LANES, ns), lambda i: (i, 0, 0))],
        out_shape=[jax.ShapeDtypeStruct((b, ns, LANES), BF16),
                   jax.ShapeDtypeStruct((b, LANES, ns), BF16)],
        compiler_params=_cparams(("parallel",)),
        name="compress_rows",
    )(x3, x3, wk, wv)


def _relayout_body(x_ref, k_ref, vt_ref, *, nchunk):
    for c in range(nchunk):
        x = x_ref[0, c * CHUNK:(c + 1) * CHUNK, :]
        k_ref[0, c * CHUNK:(c + 1) * CHUNK, :] = x[:, :LANES].astype(BF16)
        vt_ref[0, c] = x[:, LANES:].T.astype(BF16)


def kv_relayout(x3, col_block):
    b, r, _ = x3.shape
    rt = _pick(r, (1024, 512, 256, 128))
    nchunk = rt // CHUNK
    return pl.pallas_call(
        functools.partial(_relayout_body, nchunk=nchunk),
        grid=(b, r // rt),
        in_specs=[pl.BlockSpec((1, rt, 2 * LANES), lambda i, j: (i, j, col_block))],
        out_specs=[pl.BlockSpec((1, rt, LANES), lambda i, j: (i, j, 0)),
                   pl.BlockSpec((1, nchunk, LANES, LANES), lambda i, j: (i, j, 0, 0))],
        out_shape=[jax.ShapeDtypeStruct((b, r, LANES), BF16),
                   jax.ShapeDtypeStruct((b, r // CHUNK, LANES, LANES), BF16)],
        compiler_params=_cparams(("parallel", "parallel")),
        name="kv_relayout",
    )(x3)


def _attn_body(*refs, nc, nbp, k_sel, pos0, has_extra, nkc, nwc):
    if has_extra:
        (q_ref, g_ref, kc_ref, vct_ref, st_ref, ks_ref, vst_ref, kw_ref, vwt_ref,
         ksx_ref, vstx_ref, kwx_ref, vwtx_ref, o_ref, gt_ref, sc_ref, sel_ref) = refs
    else:
        (q_ref, g_ref, kc_ref, vct_ref, st_ref, ks_ref, vst_ref, kw_ref, vwt_ref,
         o_ref, gt_ref, sc_ref, sel_ref) = refs
    i = pl.program_id(1)
    lane = lax.broadcasted_iota(jnp.int32, (1, LANES), 1)
    qpos = pos0 + i * CHUNK + lane
    row = lax.broadcasted_iota(jnp.int32, (CHUNK, 1), 0)
    cur_chunk = pos0 // CHUNK + i
    qt = q_ref[0].astype(F32).T.astype(BF16)
    gt_ref[...] = jax.nn.sigmoid(g_ref[0]).T
    half = HEAD_DIM

    def softmax_update(carry, k, vt, qh, mask):
        m, l, acc = carry
        s = jnp.dot(k, qh, preferred_element_type=F32)
        s = jnp.where(mask, s, NEG_INF)
        m_new = jnp.maximum(m, jnp.max(s, axis=0, keepdims=True))
        alpha = jnp.exp(m - m_new)
        p = jnp.where(mask, jnp.exp(s - m_new), 0.0)
        l = alpha * l + jnp.sum(p, axis=0, keepdims=True)
        acc = alpha * acc + jnp.dot(vt, p.astype(BF16), preferred_element_type=F32)
        return m_new, l, acc

    def finish(carry):
        _, l, acc = carry
        return acc * (1.0 / jnp.maximum(l, 1e-30))

    init = (jnp.full((1, LANES), NEG_INF, F32), jnp.zeros((1, LANES), F32), jnp.zeros((half, LANES), F32))
    blk = lax.broadcasted_iota(jnp.int32, (nbp, 1), 0)
    cur_blk = qpos // SEL_BLOCK
    valid = blk <= cur_blk
    forced = (blk == 0) | (blk == cur_blk) | (blk == cur_blk - 1)
    cend = lax.broadcasted_iota(jnp.int32, (nc, 1), 0) * CMP_STRIDE + (CMP_LEN - 1)
    mask_c = cend <= qpos

    for g in range(N_KV):
        gsl = slice(g * half, (g + 1) * half)
        zq = jnp.zeros((half, LANES), BF16)
        qh = []
        for r in range(GROUP):
            qr = qt[(GROUP * g + r) * half:(GROUP * g + r + 1) * half, :]
            qh.append(jnp.concatenate([qr, zq] if g == 0 else [zq, qr], axis=0))
        kc = kc_ref[0]
        vct = vct_ref[0, gsl, :]
        o_c = []
        psum = jnp.zeros((nc, LANES), F32)
        for r in range(GROUP):
            s = jnp.dot(kc, qh[r], preferred_element_type=F32)
            s = jnp.where(mask_c, s, NEG_INF)
            m = jnp.max(s, axis=0, keepdims=True)
            p = jnp.where(mask_c, jnp.exp(s - m), 0.0)
            l = jnp.sum(p, axis=0, keepdims=True)
            p = p * (1.0 / jnp.maximum(l, 1e-30))
            psum = psum + p
            o_c.append(jnp.dot(vct, p.astype(BF16), preferred_element_type=F32))
        imp = jnp.dot(st_ref[...], psum, precision=HIGHEST, preferred_element_type=F32)
        score = jnp.where(valid, jnp.where(forced, FORCED_SCORE, imp), -1.0)
        sc_ref[...] = score

        def rank_step(j, rank):
            rj = sc_ref[pl.ds(j, 1), :]
            ahead = (rj > score) | ((rj == score) & (j < blk))
            return rank + jnp.where(ahead, 1.0, 0.0)

        rank = lax.fori_loop(0, nbp, rank_step, jnp.zeros((nbp, LANES), F32))
        sel_ref[...] = jnp.where(rank < k_sel, 1.0, 0.0)

        def sel_mask(c):
            s0 = sel_ref[pl.ds(2 * c, 1), :]
            s1 = sel_ref[pl.ds(2 * c + 1, 1), :]
            chosen = jnp.where(row < SEL_BLOCK, s0, s1) > 0.5
            kpos = c * CHUNK + row
            return chosen & (kpos <= qpos)

        def sel_step(c, carry):
            k = ks_ref[0, pl.ds(pl.multiple_of(c * CHUNK, CHUNK), CHUNK), :]
            vt = vst_ref[0, c, gsl, :]
            mask = sel_mask(c)
            return tuple(softmax_update(carry[r], k, vt, qh[r], mask) for r in range(GROUP))

        n_loop = nkc if has_extra else i + 1
        carry = lax.fori_loop(0, n_loop, sel_step, tuple(init for _ in range(GROUP)))
        if has_extra:
            mask = sel_mask(nkc)
            k = ksx_ref[0]
            vt = vstx_ref[0, 0, gsl, :]
            carry = tuple(softmax_update(carry[r], k, vt, qh[r], mask) for r in range(GROUP))
        o_s = [finish(carry[r]) for r in range(GROUP)]

        def win_mask(c):
            dpos = qpos - (c * CHUNK + row)
            return (dpos >= 0) & (dpos < WINDOW)

        if has_extra:
            def win_step(lc, carry):
                c = cur_chunk - nwc + lc
                k = kw_ref[0, pl.ds(pl.multiple_of(lc * CHUNK, CHUNK), CHUNK), :]
                vt = vwt_ref[0, lc, gsl, :]
                mask = win_mask(c)
                return tuple(softmax_update(carry[r], k, vt, qh[r], mask) for r in range(GROUP))

            carry = lax.fori_loop(0, nwc, win_step, tuple(init for _ in range(GROUP)))
            mask = win_mask(cur_chunk)
            k = kwx_ref[0]
            vt = vwtx_ref[0, 0, gsl, :]
            carry = tuple(softmax_update(carry[r], k, vt, qh[r], mask) for r in range(GROUP))
        else:
            def win_step(c, carry):
                k = kw_ref[0, pl.ds(pl.multiple_of(c * CHUNK, CHUNK), CHUNK), :]
                vt = vwt_ref[0, c, gsl, :]
                mask = win_mask(c)
                return tuple(softmax_update(carry[r], k, vt, qh[r], mask) for r in range(GROUP))

            first = jnp.maximum(i - WINDOW // CHUNK, 0)
            carry = lax.fori_loop(first, i + 1, win_step, tuple(init for _ in range(GROUP)))
        o_w = [finish(carry[r]) for r in range(GROUP)]

        for r in range(GROUP):
            h = GROUP * g + r
            o = (gt_ref[pl.ds(3 * h, 1), :] * o_c[r] + gt_ref[pl.ds(3 * h + 1, 1), :] * o_s[r]
                 + gt_ref[pl.ds(3 * h + 2, 1), :] * o_w[r])
            if r % 2 == 0:
                pending = o
            else:
                pair = jnp.concatenate([pending, o], axis=0).T
                o_ref[0, :, (h // 2) * LANES:(h // 2 + 1) * LANES] = pair.astype(BF16)


def nsa_attention(q3, g3, kc, vct, st, ks, vst, kw, vwt, extra, *, pos0, n_blk):
    b, tq, _ = q3.shape
    nq = tq // CHUNK
    nc = kc.shape[1]
    nbp = st.shape[0]
    nkc = vst.shape[1]
    nwc = vwt.shape[1]
    has_extra = extra is not None
    per_b3 = lambda i, j: (i, 0, 0)
    per_b4 = lambda i, j: (i, 0, 0, 0)
    in_specs = [pl.BlockSpec((1, CHUNK, Q_DIM), lambda i, j: (i, j, 0)),
                pl.BlockSpec((1, CHUNK, LANES), lambda i, j: (i, j, 0)),
                pl.BlockSpec((1, nc, LANES), per_b3),
                pl.BlockSpec((1, LANES, nc), per_b3),
                pl.BlockSpec((nbp, nc), lambda i, j: (0, 0)),
                pl.BlockSpec((1, nkc * CHUNK, LANES), per_b3),
                pl.BlockSpec((1, nkc, LANES, LANES), per_b4),
                pl.BlockSpec((1, nwc * CHUNK, LANES), per_b3),
                pl.BlockSpec((1, nwc, LANES, LANES), per_b4)]
    args = [q3, g3, kc, vct, st, ks, vst, kw, vwt]
    if has_extra:
        in_specs += [pl.BlockSpec((1, CHUNK, LANES), per_b3), pl.BlockSpec((1, 1, LANES, LANES), per_b4),
                     pl.BlockSpec((1, CHUNK, LANES), per_b3), pl.BlockSpec((1, 1, LANES, LANES), per_b4)]
        args += list(extra)
    body = functools.partial(_attn_body, nc=nc, nbp=nbp, k_sel=min(N_SEL, n_blk), pos0=pos0,
                             has_extra=has_extra, nkc=nkc, nwc=nwc)
    return pl.pallas_call(
        body,
        grid=(b, nq),
        in_specs=in_specs,
        out_specs=pl.BlockSpec((1, CHUNK, Q_DIM), lambda i, j: (i, j, 0)),
        out_shape=jax.ShapeDtypeStruct((b, tq, Q_DIM), BF16),
        scratch_shapes=[pltpu.VMEM((LANES, LANES), F32), pltpu.VMEM((nbp, LANES), F32), pltpu.VMEM((nbp, LANES), F32)],
        compiler_params=_cparams(("parallel", "arbitrary")),
        name="nsa_attention",
    )(*args)


KB = 512
WIN_KEYS = WINDOW + CHUNK
SAMPLE_QL = 32


def _tile_lanes(x, n):
    return x if n == 1 else jnp.concatenate([x] * n, axis=1)


def _flash_update(carry, s, vt, bias, ht):
    m, l, acc = carry
    s = s + _tile_lanes(bias, ht)
    m_new = jnp.maximum(m, jnp.max(s, axis=0, keepdims=True))
    alpha = jnp.exp(m - m_new)
    p = jnp.exp(s - m_new)
    l = alpha * l + jnp.sum(p, axis=0, keepdims=True)
    acc = alpha * acc + jnp.dot(vt, p.astype(BF16), preferred_element_type=F32)
    return m_new, l, acc


def _flash_init(width):
    return (jnp.full((1, width), NEG_INF, F32), jnp.zeros((1, width), F32), jnp.zeros((HEAD_DIM, width), F32))


def _flash_finish(carry):
    _, l, acc = carry
    return acc * (1.0 / l)


def _compressed_and_select(qg, kc, vct, st, qpos, sc_ref, sel_ref, *, nc, nbp, k_sel, ht, head_sum):
    cend = lax.broadcasted_iota(jnp.int32, (nc, 1), 0) * CMP_STRIDE + (CMP_LEN - 1)
    bias_c = jnp.where(cend <= qpos, 0.0, NEG_INF)
    s = jnp.dot(kc, qg, preferred_element_type=F32) + _tile_lanes(bias_c, ht)
    m = jnp.max(s, axis=0, keepdims=True)
    p = jnp.exp(s - m)
    l = jnp.sum(p, axis=0, keepdims=True)
    seen = jnp.where(qpos >= CMP_LEN - 1, 1.0, 0.0)
    p = p * (_tile_lanes(seen, ht) / l)
    o_c = jnp.dot(vct, p.astype(BF16), preferred_element_type=F32)
    psum = head_sum(p)
    imp = jnp.dot(st, psum, precision=HIGHEST, preferred_element_type=F32)
    blk = lax.broadcasted_iota(jnp.int32, (nbp, 1), 0)
    cur_blk = qpos // SEL_BLOCK
    valid = blk <= cur_blk
    forced = (blk == 0) | (blk == cur_blk) | (blk == cur_blk - 1)
    score = jnp.where(valid, jnp.where(forced, FORCED_SCORE, imp), -1.0)
    sc_ref[...] = score

    def rank_step(j, rank):
        rj = sc_ref[pl.ds(j, 1), :]
        ahead = (rj > score) | ((rj == score) & (j < blk))
        return rank + jnp.where(ahead, 1.0, 0.0)

    rank = lax.fori_loop(0, nbp, rank_step, jnp.zeros((nbp, LANES), F32), unroll=8)
    sel_ref[...] = jnp.where(rank < k_sel, 0.0, NEG_INF)
    return o_c


def _block_bias(sel_ref, first_blk, nblk):
    rows = [jnp.broadcast_to(sel_ref[pl.ds(first_blk + j, 1), :], (SEL_BLOCK, LANES)) for j in range(nblk)]
    return jnp.concatenate(rows, axis=0)


def _attn_prompt_body(q_ref, g_ref, kc_ref, vct_ref, st_ref, ks_ref, vst_ref, kw_ref, vwt_ref,
                      o_ref, gt_ref, sc_ref, sel_ref, *, nc, nbp, k_sel):
    i = pl.program_id(1)
    ht = GROUP
    kpc = KB // CHUNK
    lane = lax.broadcasted_iota(jnp.int32, (1, LANES), 1)
    qpos = i * CHUNK + lane
    qt = q_ref[0].astype(F32).T.astype(BF16)
    gt_ref[...] = jax.nn.sigmoid(g_ref[...]).T
    half = HEAD_DIM
    zq = jnp.zeros((half, ht * LANES), BF16)
    head_sum = lambda p: p[:, 0:LANES] + p[:, LANES:2 * LANES] + p[:, 2 * LANES:3 * LANES] + p[:, 3 * LANES:]
    krow = lax.broadcasted_iota(jnp.int32, (KB, 1), 0)
    wrow = lax.broadcasted_iota(jnp.int32, (WIN_KEYS, 1), 0)

    for g in range(N_KV):
        gsl = slice(g * half, (g + 1) * half)
        qrow = jnp.concatenate([qt[(GROUP * g + r) * half:(GROUP * g + r + 1) * half, :] for r in range(GROUP)], axis=1)
        qg = jnp.concatenate([qrow, zq] if g == 0 else [zq, qrow], axis=0)
        o_c = _compressed_and_select(qg, kc_ref[0], vct_ref[0, gsl, :], st_ref[...], qpos, sc_ref, sel_ref,
                                     nc=nc, nbp=nbp, k_sel=k_sel, ht=ht, head_sum=head_sum)

        def sel_tile(c):
            k = ks_ref[0, pl.ds(pl.multiple_of(c * KB, KB), KB), :]
            s = jnp.dot(k, qg, preferred_element_type=F32)
            vt = jnp.concatenate([vst_ref[0, kpc * c + j, gsl, :] for j in range(kpc)], axis=1)
            return s, vt, _block_bias(sel_ref, (KB // SEL_BLOCK) * c, KB // SEL_BLOCK)

        def sel_step(c, carry):
            s, vt, bias = sel_tile(c)
            return _flash_update(carry, s, vt, bias, ht)

        last = i // kpc
        carry = lax.fori_loop(0, last, sel_step, _flash_init(ht * LANES))
        s, vt, bias = sel_tile(last)
        bias = jnp.where(last * KB + krow <= qpos, bias, NEG_INF)
        o_s = _flash_finish(_flash_update(carry, s, vt, bias, ht))

        cs = jnp.maximum(i - WINDOW // CHUNK, 0)
        kwin = kw_ref[0, pl.ds(pl.multiple_of(cs * CHUNK, CHUNK), WIN_KEYS), :]
        s = jnp.dot(kwin, qg, preferred_element_type=F32)
        vt = jnp.concatenate([vwt_ref[0, cs + j, gsl, :] for j in range(WIN_KEYS // CHUNK)], axis=1)
        dpos = qpos - (cs * CHUNK + wrow)
        bias = jnp.where((dpos >= 0) & (dpos < WINDOW), 0.0, NEG_INF)
        o_w = _flash_finish(_flash_update(_flash_init(ht * LANES), s, vt, bias, ht))

        for r in range(GROUP):
            h = GROUP * g + r
            lsl = slice(r * LANES, (r + 1) * LANES)
            o = (gt_ref[pl.ds(3 * h, 1), :] * o_c[:, lsl] + gt_ref[pl.ds(3 * h + 1, 1), :] * o_s[:, lsl]
                 + gt_ref[pl.ds(3 * h + 2, 1), :] * o_w[:, lsl])
            if r % 2 == 0:
                pending = o
            else:
                pair = jnp.concatenate([pending, o], axis=0).T
                o_ref[0, :, (h // 2) * LANES:(h // 2 + 1) * LANES] = pair.astype(BF16)


def nsa_attention_prompt(q3, proj, kc, vct, st, ks, vst, kw, vwt, *, n_blk):
    b, tq, _ = q3.shape
    assert tq % KB == 0 and tq >= WIN_KEYS
    nq = tq // CHUNK
    nc = kc.shape[1]
    nbp = st.shape[0]
    nkc = vst.shape[1]
    per_b3 = lambda i, j: (i, 0, 0)
    per_b4 = lambda i, j: (i, 0, 0, 0)
    body = functools.partial(_attn_prompt_body, nc=nc, nbp=nbp, k_sel=min(N_SEL, n_blk))
    return pl.pallas_call(
        body,
        grid=(b, nq),
        in_specs=[pl.BlockSpec((1, CHUNK, Q_DIM), lambda i, j: (i, j, 0)),
                  pl.BlockSpec((CHUNK, LANES), lambda i, j: (i * nq + j, SM0 // LANES)),
                  pl.BlockSpec((1, nc, LANES), per_b3),
                  pl.BlockSpec((1, LANES, nc), per_b3),
                  pl.BlockSpec((nbp, nc), lambda i, j: (0, 0)),
                  pl.BlockSpec((1, nkc * CHUNK, LANES), per_b3),
                  pl.BlockSpec((1, nkc, LANES, LANES), per_b4),
                  pl.BlockSpec((1, nkc * CHUNK, LANES), per_b3),
                  pl.BlockSpec((1, nkc, LANES, LANES), per_b4)],
        out_specs=pl.BlockSpec((1, CHUNK, Q_DIM), lambda i, j: (i, j, 0)),
        out_shape=jax.ShapeDtypeStruct((b, tq, Q_DIM), BF16),
        scratch_shapes=[pltpu.VMEM((LANES, LANES), F32), pltpu.VMEM((nbp, LANES), F32), pltpu.VMEM((nbp, LANES), F32)],
        compiler_params=_cparams(("parallel", "arbitrary")),
        name="nsa_attention_prompt",
    )(q3, proj, kc, vct, st, ks, vst, kw, vwt)


def _page_copies(pages_ref, pt_ref, buf, sem, layer, seq, slot, row0, npages):
    return [pltpu.make_async_copy(pages_ref.at[layer, pt_ref[seq, p], pl.ds(row0, 2 * LANES), :],
                                  buf.at[slot, p], sem.at[slot]) for p in range(npages)]


def _fetch_pages(pages_ref, pt_ref, buf, sem, layer, row0, npages):
    b = pl.program_id(0)
    slot = lax.rem(b, 2)

    @pl.when(b == 0)
    def _():
        for cp in _page_copies(pages_ref, pt_ref, buf, sem, layer, 0, 0, row0, npages):
            cp.start()

    @pl.when(b + 1 < pl.num_programs(0))
    def _():
        for cp in _page_copies(pages_ref, pt_ref, buf, sem, layer, b + 1, 1 - slot, row0, npages):
            cp.start()

    for cp in _page_copies(pages_ref, pt_ref, buf, sem, layer, b, slot, row0, npages):
        cp.wait()
    return slot


def _cmp_sample_body(pt_ref, pages_ref, wk_ref, wv_ref, kc_ref, vct_ref, buf, sem, xk_ref, xv_ref, *, npages, layer):
    slot = _fetch_pages(pages_ref, pt_ref, buf, sem, layer, 0, npages)
    for p in range(npages):
        xk_ref[p * CHUNK:(p + 1) * CHUNK, :] = buf[slot, p, 0:LANES, :].T
        xv_ref[p * CHUNK:(p + 1) * CHUNK, :] = buf[slot, p, LANES:2 * LANES, :].T
    ns = npages * CHUNK // CMP_STRIDE
    ak = jnp.zeros((ns, 2 * LANES), F32)
    av = jnp.zeros((ns, 2 * LANES), F32)
    for j in range(CMP_STRIDE):
        rk = xk_ref[pl.ds(j, ns, stride=CMP_STRIDE), :]
        rv = xv_ref[pl.ds(j, ns, stride=CMP_STRIDE), :]
        ak = ak + jnp.dot(rk.astype(BF16), wk_ref[j], preferred_element_type=F32)
        av = av + jnp.dot(rv.astype(BF16), wv_ref[j], preferred_element_type=F32)
    kc = ak[:, :LANES] + pltpu.roll(ak[:, LANES:], ns - 1, 0)
    vc = av[:, :LANES] + pltpu.roll(av[:, LANES:], ns - 1, 0)
    kc_ref[0] = kc.astype(BF16)
    vct_ref[0] = vc.T.astype(BF16)


def compress_pages(pages_t, page_table, layer, wk, wv):
    nb, npages = page_table.shape
    ns = npages * CHUNK // CMP_STRIDE
    grid_spec = pltpu.PrefetchScalarGridSpec(
        num_scalar_prefetch=1,
        grid=(nb,),
        in_specs=[pl.BlockSpec(memory_space=pl.ANY),
                  pl.BlockSpec((CMP_STRIDE, LANES, 2 * LANES), lambda i, pt: (0, 0, 0)),
                  pl.BlockSpec((CMP_STRIDE, LANES, 2 * LANES), lambda i, pt: (0, 0, 0))],
        out_specs=[pl.BlockSpec((1, ns, LANES), lambda i, pt: (i, 0, 0)),
                   pl.BlockSpec((1, LANES, ns), lambda i, pt: (i, 0, 0))],
        scratch_shapes=[pltpu.VMEM((2, npages, 2 * LANES, LANES), F32), pltpu.SemaphoreType.DMA((2,)),
                        pltpu.VMEM((npages * CHUNK, LANES), F32), pltpu.VMEM((npages * CHUNK, LANES), F32)],
    )
    return pl.pallas_call(
        functools.partial(_cmp_sample_body, npages=npages, layer=layer),
        grid_spec=grid_spec,
        out_shape=[jax.ShapeDtypeStruct((nb, ns, LANES), BF16), jax.ShapeDtypeStruct((nb, LANES, ns), BF16)],
        compiler_params=_cparams(("arbitrary",)),
        name="compress_pages",
    )(page_table, pages_t, wk, wv)


def _attn_sample_body(pt_ref, pages_ref, qg_ref, gs_ref, kc_ref, vct_ref, st_ref, wc_ref, kn_ref, vnt_ref,
                      kwn_ref, vwnt_ref, o_ref, buf, sem, sc_ref, sel_ref, *, npages, layer, nc, nbp, k_sel, pos0):
    slot = _fetch_pages(pages_ref, pt_ref, buf, sem, layer, 2 * LANES, npages)
    half = HEAD_DIM
    kpc = KB // CHUNK
    lane = lax.broadcasted_iota(jnp.int32, (1, LANES), 1)
    qpos = pos0 + lane % SAMPLE_QL
    head_sum = lambda p: (p + pltpu.roll(p, SAMPLE_QL, 1) + pltpu.roll(p, 2 * SAMPLE_QL, 1)
                          + pltpu.roll(p, 3 * SAMPLE_QL, 1))
    qgs = [qg_ref[0, g] for g in range(N_KV)]
    o_c = []
    for g in range(N_KV):
        o_c.append(_compressed_and_select(qgs[g], kc_ref[0], vct_ref[0, g * half:(g + 1) * half, :], st_ref[...],
                                          qpos, sc_ref, sel_ref.at[g], nc=nc, nbp=nbp, k_sel=k_sel, ht=1,
                                          head_sum=head_sum))
    row = lax.broadcasted_iota(jnp.int32, (CHUNK, 1), 0)
    new_vis = (pos0 + row) <= qpos

    def sel_step(c, carry):
        k = jnp.concatenate([buf[slot, kpc * c + j, 0:LANES, :].T for j in range(kpc)], axis=0).astype(BF16)
        out = []
        for g in range(N_KV):
            s = jnp.dot(k, qgs[g], preferred_element_type=F32)
            vt = jnp.concatenate([buf[slot, kpc * c + j, pl.ds(LANES + g * half, half), :] for j in range(kpc)],
                                 axis=1).astype(BF16)
            bias = _block_bias(sel_ref.at[g], (KB // SEL_BLOCK) * c, KB // SEL_BLOCK)
            out.append(_flash_update(carry[g], s, vt, bias, 1))
        return tuple(out)

    carry = lax.fori_loop(0, npages // kpc, sel_step, tuple(_flash_init(LANES) for _ in range(N_KV)))
    o_s = []
    for g in range(N_KV):
        s = jnp.dot(kn_ref[0], qgs[g], preferred_element_type=F32)
        bias = jnp.where(new_vis, _block_bias(sel_ref.at[g], npages * CHUNK // SEL_BLOCK, CHUNK // SEL_BLOCK), NEG_INF)
        o_s.append(_flash_finish(_flash_update(carry[g], s, vnt_ref[0, g * half:(g + 1) * half, :], bias, 1)))

    lw = wc_ref.shape[3]
    kw = jnp.concatenate([wc_ref[0, 0, 0:LANES, j * CHUNK:(j + 1) * CHUNK].T for j in range(lw // CHUNK)],
                         axis=0).astype(BF16)
    wrow = lax.broadcasted_iota(jnp.int32, (lw, 1), 0)
    bias_w = jnp.where(qpos - (pos0 - lw + wrow) < WINDOW, 0.0, NEG_INF)
    bias_n = jnp.where(new_vis, 0.0, NEG_INF)
    for g in range(N_KV):
        gsl = slice(g * half, (g + 1) * half)
        s = jnp.dot(kw, qgs[g], preferred_element_type=F32)
        vt = wc_ref[0, 0, pl.ds(LANES + g * half, half), :].astype(BF16)
        carry_w = _flash_update(_flash_init(LANES), s, vt, bias_w, 1)
        s = jnp.dot(kwn_ref[0], qgs[g], preferred_element_type=F32)
        o_w = _flash_finish(_flash_update(carry_w, s, vwnt_ref[0, gsl, :], bias_n, 1))
        gates = jax.nn.sigmoid(gs_ref[0, g])
        o_ref[0, g] = gates[0:1, :] * o_c[g] + gates[1:2, :] * o_s[g] + gates[2:3, :] * o_w


def nsa_attention_sample(pages_t, wcache_t, page_table, layer, qg, gs, kc, vct, st, kn, vnt, kwn, vwnt, *, pos0, n_blk):
    nb, npages = page_table.shape
    lw = wcache_t.shape[3]
    nc = kc.shape[1]
    nbp = st.shape[0]
    assert npages % (KB // CHUNK) == 0 and pos0 == npages * CHUNK and lw == WINDOW
    per_b3 = lambda i, pt: (i, 0, 0)
    per_b4 = lambda i, pt: (i, 0, 0, 0)
    grid_spec = pltpu.PrefetchScalarGridSpec(
        num_scalar_prefetch=1,
        grid=(nb,),
        in_specs=[pl.BlockSpec(memory_space=pl.ANY),
                  pl.BlockSpec((1, N_KV, LANES, LANES), per_b4),
                  pl.BlockSpec((1, N_KV, 8, LANES), per_b4),
                  pl.BlockSpec((1, nc, LANES), per_b3),
                  pl.BlockSpec((1, LANES, nc), per_b3),
                  pl.BlockSpec((nbp, nc), lambda i, pt: (0, 0)),
                  pl.BlockSpec((1, 1, 2 * LANES, lw), lambda i, pt: (layer, i, 0, 0)),
                  pl.BlockSpec((1, CHUNK, LANES), per_b3), pl.BlockSpec((1, LANES, CHUNK), per_b3),
                  pl.BlockSpec((1, CHUNK, LANES), per_b3), pl.BlockSpec((1, LANES, CHUNK), per_b3)],
        out_specs=pl.BlockSpec((1, N_KV, HEAD_DIM, LANES), per_b4),
        scratch_shapes=[pltpu.VMEM((2, npages, 2 * LANES, LANES), F32), pltpu.SemaphoreType.DMA((2,)),
                        pltpu.VMEM((nbp, LANES), F32), pltpu.VMEM((N_KV, nbp, LANES), F32)],
    )
    body = functools.partial(_attn_sample_body, npages=npages, layer=layer, nc=nc, nbp=nbp,
                             k_sel=min(N_SEL, n_blk), pos0=pos0)
    return pl.pallas_call(
        body,
        grid_spec=grid_spec,
        out_shape=jax.ShapeDtypeStruct((nb, N_KV, HEAD_DIM, LANES), F32),
        compiler_params=_cparams(("arbitrary",)),
        name="nsa_attention_sample",
    )(page_table, pages_t, qg, gs, kc, vct, st, wcache_t, kn, vnt, kwn, vwnt)


def _bc_body(xbc_ref, z_ref, sm_ref, scb_ref, scc_ref, sch_ref, h0_ref, tssm_ref, tsc_ref,
             cw_ref, cb_ref, dtb_ref, a_ref, dw_ref, nw_ref, scw_ref, ex_ref, ext_ref, tri_ref, trit_ref,
             ossm_ref, osc_ref, ht_ref, xp_ref, cp_ref, h_ref, *, valid_len):
    c = pl.program_id(1)
    nchunks = pl.num_programs(1)
    L = CHUNK

    @pl.when(c == 0)
    def _():
        xp_ref[0:8, :] = tssm_ref[0]
        cp_ref[0:8, :] = tsc_ref[0]
        h_ref[...] = h0_ref[0]

    x = xbc_ref[...]
    xp_ref[8:8 + L, :] = x
    conv = cb_ref[...] + cw_ref[0:1, :] * xp_ref[pl.ds(8 - 3, L), :]
    for k in range(1, SSM_CONV):
        conv = conv + cw_ref[k:k + 1, :] * xp_ref[pl.ds(8 - 3 + k, L), :]
    xp_ref[0:8, :] = x[L - 8:L, :]
    xbc = _silu(conv)
    xs = xbc[:, :SSM_INNER]
    bm = xbc[:, SSM_INNER:SSM_INNER + LANES]
    cm = xbc[:, SSM_INNER + LANES:]

    ch = scc_ref[...] * sch_ref[...]
    cp_ref[8:8 + L, :] = ch
    sc = scw_ref[0:1, :] * cp_ref[pl.ds(8 - 2, L), :]
    for k in range(1, SC_WIDTH):
        sc = sc + scw_ref[k:k + 1, :] * cp_ref[pl.ds(8 - 2 + k, L), :]
    cp_ref[0:8, :] = ch[L - 8:L, :]
    osc_ref[...] = (scb_ref[...] * sc).astype(osc_ref.dtype)

    raw = sm_ref[...] + dtb_ref[...]
    dt = jnp.maximum(raw, 0.0) + jnp.log1p(jnp.exp(-jnp.abs(raw)))
    if valid_len < L:
        trow = lax.broadcasted_iota(jnp.int32, (L, 1), 0)
        dt = jnp.where(trow < valid_len, dt, 0.0)
    da = dt * a_ref[...]
    acum = jnp.dot(tri_ref[...], da, precision=HIGHEST, preferred_element_type=F32)
    acum_t = jnp.dot(da.T, trit_ref[...], precision=HIGHEST, preferred_element_type=F32)
    ex = ex_ref[...]
    dt_w = jnp.dot(dt, ex, precision=HIGHEST, preferred_element_type=F32)
    acum_w = jnp.dot(acum, ex, precision=HIGHEST, preferred_element_type=F32)
    last_w = acum_w[L - 1:L, :]
    xdt = xs * dt_w
    xdt_b = xdt.astype(BF16)
    xtail_t = (xdt * jnp.exp(last_w - acum_w)).T.astype(BF16)
    eacum_w = jnp.exp(acum_w)
    acum_wt = jnp.dot(ext_ref[...], acum_t, precision=HIGHEST, preferred_element_type=F32)
    hdecay = jnp.exp(acum_wt[:, L - 1:L])

    lane = lax.broadcasted_iota(jnp.int32, (1, LANES), 1)
    trow2 = lax.broadcasted_iota(jnp.int32, (L, L), 0)
    tcol2 = lax.broadcasted_iota(jnp.int32, (L, L), 1)
    causal = tcol2 <= trow2
    ys = []
    heads_per_group = SSM_HEADS // SSM_GROUPS
    for g in range(SSM_GROUPS):
        gmask = (lane // SSM_STATE) == g
        cm_g = jnp.where(gmask, cm, 0.0).astype(BF16)
        bm_g = jnp.where(gmask, bm, 0.0).astype(BF16)
        cb = lax.dot_general(cm_g, bm_g, (((1,), (1,)), ((), ())), preferred_element_type=F32)
        for pair in range(heads_per_group // 2):
            pcol = g * (heads_per_group // 2) + pair
            psl = slice(pcol * LANES, (pcol + 1) * LANES)
            xpair = xdt_b[:, psl]
            y = jnp.zeros((L, LANES), F32)
            for e in range(2):
                h = 2 * pcol + e
                ac = acum[:, DT_LANE0 + h:DT_LANE0 + h + 1]
                ar = acum_t[DT_LANE0 + h:DT_LANE0 + h + 1, :]
                decay = jnp.exp(jnp.where(causal, ac - ar, NEG_INF))
                emask = (lane // SSM_HEAD_DIM) == e
                xh = jnp.where(emask, xpair, jnp.zeros_like(xpair))
                y = y + jnp.dot((cb * decay).astype(BF16), xh, preferred_element_type=F32)
            hp = h_ref[pcol * LANES:(pcol + 1) * LANES, :]
            y_off = lax.dot_general(cm_g, hp.astype(BF16), (((1,), (1,)), ((), ())), preferred_element_type=F32)
            ys.append(y + y_off * eacum_w[:, psl])
        rsl = slice(g * heads_per_group * SSM_HEAD_DIM, (g + 1) * heads_per_group * SSM_HEAD_DIM)
        upd = jnp.dot(xtail_t[rsl, :], bm_g, preferred_element_type=F32)
        h_ref[rsl, :] = h_ref[rsl, :] * hdecay[rsl, :] + upd

    y = jnp.concatenate(ys, axis=1)
    y = (y + xs * dw_ref[...]) * _silu(z_ref[...])
    ossm_ref[...] = _rms(y, nw_ref[...]).astype(ossm_ref.dtype)

    @pl.when(c == nchunks - 1)
    def _():
        ht_ref[0] = h_ref[...]


def ssd_shortconv(srcs, row_block0, nb, nchunks, h0, tail_ssm, tail_sc, consts, valid_len):
    cw, cb, dtb, a_full, dw, nw, scw, ex, ext, tri, trit = consts
    widths = (SSM_CONV_DIM, SSM_INNER, LANES, SC_DIM, SC_DIM, SC_DIM)
    in_specs, args = [], []
    for (arr, cblk), w in zip(srcs, widths):
        in_specs.append(pl.BlockSpec((CHUNK, w), functools.partial(
            lambda b, c, cblk: (row_block0 + b * nchunks + c, cblk), cblk=cblk)))
        args.append(arr)
    hp = SSM_HEADS * SSM_HEAD_DIM
    in_specs += [pl.BlockSpec((1, hp, LANES), lambda b, c: (b, 0, 0)),
                 pl.BlockSpec((1, 8, SSM_CONV_DIM), lambda b, c: (b, 0, 0)),
                 pl.BlockSpec((1, 8, SC_DIM), lambda b, c: (b, 0, 0))]
    args += [h0, tail_ssm, tail_sc]
    for cst in (cw, cb, dtb, a_full, dw, nw, scw, ex, ext, tri, trit):
        in_specs.append(pl.BlockSpec(cst.shape, lambda b, c: (0, 0)))
        args.append(cst)
    rows = nb * nchunks * CHUNK
    return pl.pallas_call(
        functools.partial(_bc_body, valid_len=valid_len),
        grid=(nb, nchunks),
        in_specs=in_specs,
        out_specs=[pl.BlockSpec((CHUNK, SSM_INNER), lambda b, c: (b * nchunks + c, 0)),
                   pl.BlockSpec((CHUNK, SC_DIM), lambda b, c: (b * nchunks + c, 0)),
                   pl.BlockSpec((1, hp, LANES), lambda b, c: (b, 0, 0))],
        out_shape=[jax.ShapeDtypeStruct((rows, SSM_INNER), BF16),
                   jax.ShapeDtypeStruct((rows, SC_DIM), BF16),
                   jax.ShapeDtypeStruct((nb, hp, LANES), F32)],
        scratch_shapes=[pltpu.VMEM((8 + CHUNK, SSM_CONV_DIM), F32), pltpu.VMEM((8 + CHUNK, SC_DIM), F32),
                        pltpu.VMEM((hp, LANES), F32)],
        compiler_params=_cparams(("parallel", "arbitrary")),
        name="ssd_shortconv",
    )(*args)


def _merge_body(on_ref, os_ref, oc_ref, g0_ref, g1_ref, g2_ref, x_ref, wb_ref, wo_ref, o_ref):
    y = jax.nn.sigmoid(g0_ref[...]) * jnp.dot(on_ref[...], wb_ref[0], preferred_element_type=F32)
    y = y + jax.nn.sigmoid(g1_ref[...]) * jnp.dot(os_ref[...], wb_ref[1], preferred_element_type=F32)
    y = y + jax.nn.sigmoid(g2_ref[...]) * jnp.dot(oc_ref[...], wb_ref[2], preferred_element_type=F32)
    o_ref[...] = x_ref[...] + jnp.dot(y.astype(BF16), wo_ref[...], preferred_element_type=F32)


def merge_out(o_nsa, o_ssm, o_sc, proj, x, wb, wo):
    t, d = x.shape
    tm = _pick(t, (640, 512, 384, 256, 128))
    row = lambda i: (i, 0)
    mg = MG0 // D_MODEL
    return pl.pallas_call(
        _merge_body,
        grid=(t // tm,),
        in_specs=[pl.BlockSpec((tm, BRANCH_DIM), row), pl.BlockSpec((tm, BRANCH_DIM), row),
                  pl.BlockSpec((tm, BRANCH_DIM), row),
                  pl.BlockSpec((tm, d), lambda i: (i, mg)), pl.BlockSpec((tm, d), lambda i: (i, mg + 1)),
                  pl.BlockSpec((tm, d), lambda i: (i, mg + 2)),
                  pl.BlockSpec((tm, d), row),
                  pl.BlockSpec((N_BRANCH, BRANCH_DIM, d), lambda i: (0, 0, 0)),
                  pl.BlockSpec((d, d), lambda i: (0, 0))],
        out_specs=pl.BlockSpec((tm, d), row),
        out_shape=jax.ShapeDtypeStruct((t, d), F32),
        compiler_params=_cparams(("parallel",)),
        name="merge_out",
    )(o_nsa, o_ssm, o_sc, proj, proj, proj, x, wb, wo)


def _mlp_body(te_ref, x_ref, nw_ref, wg_ref, wu_ref, wd_ref, sc_ref, o_ref, xn_ref, acc_ref, *, dense):
    f = pl.program_id(1)

    @pl.when(f == 0)
    def _():
        if dense:
            xn_ref[...] = _rms(x_ref[...], nw_ref[...]).astype(BF16)
        else:
            xn_ref[...] = x_ref[...]
        acc_ref[...] = jnp.zeros_like(acc_ref)

    xn = xn_ref[...]
    gate = jnp.dot(xn, wg_ref[0], preferred_element_type=F32)
    up = jnp.dot(xn, wu_ref[0], preferred_element_type=F32)
    act = (_silu(gate) * up).astype(BF16)
    acc_ref[...] += jnp.dot(act, wd_ref[0], preferred_element_type=F32)

    @pl.when(f == pl.num_programs(1) - 1)
    def _():
        if dense:
            o_ref[...] = x_ref[...] + acc_ref[...]
        else:
            o_ref[...] = sc_ref[...] * acc_ref[...]


def swiglu_mlp(x, nw, wg, wu, wd, tile_expert, row_scale, *, dense, tm):
    r, d = x.shape
    ff = wg.shape[2]
    tf = _pick(ff, (1408, 896, 512, 256, 128))
    grid_spec = pltpu.PrefetchScalarGridSpec(
        num_scalar_prefetch=1,
        grid=(r // tm, ff // tf),
        in_specs=[pl.BlockSpec((tm, d), lambda i, f, te: (i, 0)),
                  pl.BlockSpec((1, d), lambda i, f, te: (0, 0)),
                  pl.BlockSpec((1, d, tf), lambda i, f, te: (te[i], 0, f)),
                  pl.BlockSpec((1, d, tf), lambda i, f, te: (te[i], 0, f)),
                  pl.BlockSpec((1, tf, d), lambda i, f, te: (te[i], f, 0)),
                  pl.BlockSpec((tm, 1), lambda i, f, te: (i, 0))],
        out_specs=pl.BlockSpec((tm, d), lambda i, f, te: (i, 0)),
        scratch_shapes=[pltpu.VMEM((tm, d), BF16), pltpu.VMEM((tm, d), F32)],
    )
    return pl.pallas_call(
        functools.partial(_mlp_body, dense=dense),
        grid_spec=grid_spec,
        out_shape=jax.ShapeDtypeStruct((r, d), F32),
        compiler_params=_cparams(("parallel", "arbitrary")),
        name="swiglu_dense" if dense else "swiglu_grouped",
    )(tile_expert, x, nw.reshape(1, d), wg, wu, wd, row_scale)


def _router_body(x_ref, nw_ref, wr_ref, hn_ref, lg_ref):
    hn = _rms(x_ref[...], nw_ref[...])
    hn_ref[...] = hn.astype(BF16)
    lg_ref[...] = jnp.dot(hn, wr_ref[...], precision=HIGHEST, preferred_element_type=F32)


def moe_router(x, nw, wr_pad):
    t, d = x.shape
    tm = _pick(t, (640, 512, 384, 256, 128))
    return pl.pallas_call(
        _router_body,
        grid=(t // tm,),
        in_specs=[pl.BlockSpec((tm, d), lambda i: (i, 0)), pl.BlockSpec((1, d), lambda i: (0, 0)),
                  pl.BlockSpec((d, LANES), lambda i: (0, 0))],
        out_specs=[pl.BlockSpec((tm, d), lambda i: (i, 0)), pl.BlockSpec((tm, LANES), lambda i: (i, 0))],
        out_shape=[jax.ShapeDtypeStruct((t, d), BF16), jax.ShapeDtypeStruct((t, LANES), F32)],
        compiler_params=_cparams(("parallel",)),
        name="moe_router",
    )(x, nw.reshape(1, d), wr_pad)


def moe_ffn(x, nw, router, wg, wu, wd, tm):
    t, d = x.shape
    ne = router.shape[1]
    hn, logits = moe_router(x, nw, jnp.pad(router, ((0, 0), (0, LANES - ne))))
    top_v, top_i = lax.top_k(logits[:, :ne], TOP_K)
    gate = jax.nn.softmax(top_v, axis=-1)
    npair = t * TOP_K
    e_flat = top_i.reshape(npair)
    onehot = (e_flat[:, None] == jnp.arange(ne)[None, :]).astype(jnp.int32)
    csum = jnp.cumsum(onehot, axis=0)
    counts = csum[-1]
    rank = jnp.take_along_axis(csum, e_flat[:, None], axis=1)[:, 0] - 1
    tiles_e = (counts + tm - 1) // tm
    tile_end = jnp.cumsum(tiles_e)
    tile_start = tile_end - tiles_e
    n_tiles = npair // tm + ne
    dest = tile_start[e_flat] * tm + rank
    tile_ids = jnp.arange(n_tiles)
    tile_expert = jnp.minimum(jnp.sum(tile_ids[:, None] >= tile_end[None, :], axis=1), ne - 1).astype(jnp.int32)
    order = jnp.argsort(e_flat, stable=True)
    cnt_start = jnp.cumsum(counts) - counts
    rows = jnp.arange(n_tiles * tm)
    row_e = jnp.repeat(tile_expert, tm)
    within = rows - jnp.repeat(tile_start[tile_expert], tm) * tm
    row_valid = (within < counts[row_e]) & (jnp.repeat(tile_ids, tm) < tile_end[ne - 1])
    src_pair = order[jnp.clip(cnt_start[row_e] + within, 0, npair - 1)]
    xg = jnp.take(hn, src_pair // TOP_K, axis=0)
    scale = jnp.where(row_valid, gate.reshape(npair)[src_pair], 0.0).astype(F32)[:, None]
    y = swiglu_mlp(xg, nw, wg, wu, wd, tile_expert, scale, dense=False, tm=tm)
    contrib = jnp.take(y, dest, axis=0).reshape(t, TOP_K, d)
    return x + contrib[:, 0] + contrib[:, 1]


def _perm_w_in(w):
    d = w.shape[0]
    seg = lambda o, n: w[:, o:o + n]
    small = jnp.concatenate([seg(_OG, 3 * N_HEADS), seg(_ODT, SSM_HEADS),
                             jnp.zeros((d, LANES - 3 * N_HEADS - SSM_HEADS), w.dtype)], axis=1)
    out = jnp.concatenate([seg(_OQ, Q_DIM), seg(_OZ, SSM_INNER), seg(_OSCB, SC_DIM), seg(_OKV, 6 * KV_DIM),
                           seg(_OXBC, SSM_CONV_DIM), seg(_OSCC, SC_DIM), seg(_OSCH, SC_DIM),
                           seg(_OMG, N_BRANCH * D_MODEL), small,
                           jnp.zeros((d, N_PROJ - SM0 - LANES), w.dtype)], axis=1)
    return out.astype(BF16)


def _cmp_weights(w_cmp_l):
    wh = w_cmp_l.reshape(2, 2, CMP_STRIDE, HEAD_DIM, HEAD_DIM)
    eye = jnp.eye(N_KV, dtype=w_cmp_l.dtype)
    big = jnp.einsum('ahjde,gG->ajgdhGe', wh, eye)
    big = big.reshape(2, CMP_STRIDE, N_KV * HEAD_DIM, 2 * N_KV * HEAD_DIM).astype(BF16)
    return big[0], big[1]


def _rope_tables(pos):
    halfd = HEAD_DIM // 2
    inv_freq = jnp.exp(-math.log(ROPE_THETA) * jnp.arange(halfd, dtype=F32) / halfd)
    ang = pos.astype(F32)[:, None] * inv_freq[None, :]
    cos, sin = jnp.cos(ang), jnp.sin(ang)
    cos_t = jnp.concatenate([cos, cos, cos, cos], axis=1)
    sin_t = jnp.concatenate([-sin, sin, -sin, sin], axis=1)
    return cos_t, sin_t


def _block_sum_matrix(n_blk, nbp, nc):
    ratio = SEL_BLOCK // CMP_STRIDE
    b = jnp.arange(nbp)[:, None]
    c = jnp.arange(nc)[None, :]
    return ((c // ratio == b) & (b < n_blk)).astype(F32)


def _pad_rows(x, n):
    return jnp.pad(x, ((0, 0), (0, n - x.shape[1])) + ((0, 0),) * (x.ndim - 2))


def kernel(x_prompt, x_sample, cache_nsa_pages, cache_nsa_window, state_ssm, state_ssm_conv, state_shortconv, page_table, norm_mix, w_in, q_norm, k_norm, w_cmp, ssm_conv_w, ssm_conv_b, ssm_dt_bias, ssm_a_log, ssm_d, ssm_norm, sc_conv_w, w_branch, w_out, norm_ffn, ffn_w_gate, ffn_w_up, ffn_w_down, moe_router, moe_w_gate, moe_w_up, moe_w_down):
    bp, tp, d = x_prompt.shape
    bd, td, _ = x_sample.shape
    depth = w_in.shape[0]
    page = cache_nsa_pages.shape[2]
    past_len = page_table.shape[1] * page
    n_tp, n_ts = bp * tp, bd * SAMPLE_SLOT
    lw = cache_nsa_window.shape[2]
    assert tp % CHUNK == 0 and past_len % CHUNK == 0 and lw % CHUNK == 0 and td <= SAMPLE_SLOT and lw == WINDOW
    assert td < CMP_STRIDE and past_len % CMP_STRIDE == 0
    nqp = tp // CHUNK

    xs_pad = _pad_rows(x_sample, SAMPLE_SLOT).reshape(n_ts, d)
    x = jnp.concatenate([x_prompt.reshape(n_tp, d), xs_pad], axis=0)

    cos_p, sin_p = _rope_tables(jnp.arange(tp, dtype=jnp.int32))
    cos_s, sin_s = _rope_tables(jnp.tile(past_len + jnp.arange(SAMPLE_SLOT, dtype=jnp.int32), bd))
    pages_t = jnp.transpose(cache_nsa_pages, (0, 1, 3, 4, 5, 2)).reshape(depth, -1, 4 * KV_DIM, page)
    wcache_t = jnp.transpose(cache_nsa_window, (0, 1, 3, 4, 5, 2)).reshape(depth, bd, 2 * KV_DIM, lw)
    lane = jnp.arange(LANES)
    bd_mat = ((lane[:, None] // HEAD_DIM) == (lane[None, :] // HEAD_DIM)).astype(F32) / HEAD_DIM
    n_blk_p = -(-tp // SEL_BLOCK)
    nc_p = tp // CMP_STRIDE
    st_p = _block_sum_matrix(n_blk_p, -(-n_blk_p // 8) * 8, nc_p)
    n_blk_s = -(-(past_len + td) // SEL_BLOCK)
    nc_s = past_len // CMP_STRIDE
    st_s = _block_sum_matrix(n_blk_s, -(-(n_blk_s + 1) // 8) * 8, nc_s)
    hl = jnp.arange(SSM_HEADS)
    ex = jnp.zeros((LANES, SSM_INNER), F32).at[DT_LANE0 + jnp.repeat(hl, SSM_HEAD_DIM), jnp.arange(SSM_INNER)].set(1.0)
    tri = (jnp.arange(CHUNK)[None, :] <= jnp.arange(CHUNK)[:, None]).astype(F32)
    trit = tri.T
    zero_h = jnp.zeros((bp, SSM_HEADS * SSM_HEAD_DIM, LANES), F32)
    zero_tssm = jnp.zeros((bp, 8, SSM_CONV_DIM), F32)
    zero_tsc = jnp.zeros((bp, 8, SC_DIM), F32)
    heads_per_group = SSM_HEADS // SSM_GROUPS

    def lanes_24(v):
        return jnp.zeros((1, LANES), F32).at[0, DT_LANE0:DT_LANE0 + SSM_HEADS].set(v.astype(F32))

    def state_to_kernel(h):
        b = h.shape[0]
        hg = h.reshape(b, SSM_GROUPS, heads_per_group * SSM_HEAD_DIM, SSM_STATE)
        out = jnp.zeros((b, SSM_GROUPS, heads_per_group * SSM_HEAD_DIM, SSM_GROUPS, SSM_STATE), F32)
        for g in range(SSM_GROUPS):
            out = out.at[:, g, :, g, :].set(hg[:, g])
        return out.reshape(b, SSM_HEADS * SSM_HEAD_DIM, LANES)

    def state_from_kernel(hk):
        b = hk.shape[0]
        h5 = hk.reshape(b, SSM_GROUPS, heads_per_group * SSM_HEAD_DIM, SSM_GROUPS, SSM_STATE)
        hg = jnp.stack([h5[:, g, :, g, :] for g in range(SSM_GROUPS)], axis=1)
        return hg.reshape(b, SSM_HEADS, SSM_HEAD_DIM, SSM_STATE)

    outs = {k: [] for k in ("rows_p", "rows_s", "win_p", "win_s", "ssm_p", "ssm_s", "sconv_p", "sconv_s",
                            "cconv_p", "cconv_s")}
    tm_moe = 512
    for l in range(depth):
        proj = norm_matmul(x, norm_mix[l], _perm_w_in(w_in[l]))
        qw = jnp.tile(q_norm[l], 2).reshape(1, LANES)
        kw = jnp.pad(jnp.tile(k_norm[l], (1, 2)), ((0, 5), (0, 0)))
        wk, wv = _cmp_weights(w_cmp[l])

        q_p, cmp_p, ks_p, kw_p, vst_p, vwt_p, rows_pt, win_pt = nsa_prep_prompt(
            proj, bp, tp, cos_p, sin_p, qw, kw, bd_mat)
        kc_p, vct_p = compress_rows(cmp_p.reshape(bp, tp, 2 * LANES), 0, wk, wv)
        o_nsa_p = nsa_attention_prompt(q_p.reshape(bp, tp, Q_DIM), proj, kc_p, vct_p, st_p,
                                       ks_p.reshape(bp, tp, LANES), vst_p, kw_p.reshape(bp, tp, LANES), vwt_p,
                                       n_blk=n_blk_p)

        proj_s = proj[n_tp:]
        q_s, rows_s, win_s = nsa_prep(proj_s, cos_s, sin_s, qw, kw, bd_mat)
        rows_s3 = rows_s.reshape(bd, SAMPLE_SLOT, 4 * KV_DIM)
        win_s3 = win_s.reshape(bd, SAMPLE_SLOT, 2 * KV_DIM)
        tmask = (jnp.arange(SAMPLE_SLOT) < td)[None, :, None]
        new_r = _pad_rows(jnp.where(tmask, rows_s3, 0.0), CHUNK).astype(BF16)
        new_w = _pad_rows(jnp.where(tmask, win_s3, 0.0), CHUNK).astype(BF16)
        kn, vnt = new_r[:, :, 2 * LANES:3 * LANES], jnp.swapaxes(new_r[:, :, 3 * LANES:], 1, 2)
        kwn, vwnt = new_w[:, :, :LANES], jnp.swapaxes(new_w[:, :, LANES:], 1, 2)
        q5 = q_s.reshape(bd, SAMPLE_SLOT, N_KV, GROUP, HEAD_DIM).transpose(0, 2, 4, 3, 1)
        q5 = jnp.pad(q5, ((0, 0),) * 4 + ((0, SAMPLE_QL - SAMPLE_SLOT),)).reshape(bd, N_KV, HEAD_DIM, LANES)
        zq = jnp.zeros_like(q5[:, 0])
        qg = jnp.stack([jnp.concatenate([q5[:, 0], zq], axis=1), jnp.concatenate([zq, q5[:, 1]], axis=1)], axis=1)
        g5 = proj_s[:, SM0:SM0 + 3 * N_HEADS].reshape(bd, SAMPLE_SLOT, N_KV, GROUP, 3).transpose(0, 2, 4, 3, 1)
        g5 = jnp.pad(g5, ((0, 0),) * 4 + ((0, SAMPLE_QL - SAMPLE_SLOT),)).reshape(bd, N_KV, 3, LANES)
        gs = jnp.pad(g5, ((0, 0), (0, 0), (0, 8 - 3), (0, 0)))
        kc_s, vct_s = compress_pages(pages_t, page_table, l, wk, wv)
        o_s = nsa_attention_sample(pages_t, wcache_t, page_table, l, qg, gs, kc_s, vct_s, st_s, kn, vnt, kwn, vwnt,
                                   pos0=past_len, n_blk=n_blk_s)
        o_s = o_s.reshape(bd, N_KV, HEAD_DIM, GROUP, SAMPLE_QL)[..., :SAMPLE_SLOT].transpose(0, 4, 1, 3, 2)
        o_nsa = jnp.concatenate([o_nsa_p.reshape(n_tp, Q_DIM), o_s.reshape(n_ts, Q_DIM).astype(BF16)], axis=0)

        consts = (ssm_conv_w[l], ssm_conv_b[l].reshape(1, -1), lanes_24(ssm_dt_bias[l]),
                  lanes_24(-jnp.exp(ssm_a_log[l].astype(F32))),
                  jnp.repeat(ssm_d[l].astype(F32), SSM_HEAD_DIM).reshape(1, -1), ssm_norm[l].reshape(1, -1),
                  sc_conv_w[l], ex, ex.T, tri, trit)
        col_blocks = (XBC0 // SSM_CONV_DIM, Z0 // SSM_INNER, SM0 // LANES, SCB0 // SC_DIM, SCC0 // SC_DIM,
                      SCH0 // SC_DIM)
        o_ssm_p, o_sc_p, h_p = ssd_shortconv([(proj, cb) for cb in col_blocks], 0, bp, nqp,
                                             zero_h, zero_tssm, zero_tsc, consts, CHUNK)
        proj_s = proj_s.reshape(bd, SAMPLE_SLOT, N_PROJ)
        seg_s = lambda o, w: _pad_rows(proj_s[:, :, o:o + w], CHUNK).reshape(bd * CHUNK, w)
        srcs_s = [(seg_s(XBC0, SSM_CONV_DIM), 0), (seg_s(Z0, SSM_INNER), 0), (seg_s(SM0, LANES), 0),
                  (seg_s(SCB0, SC_DIM), 0), (seg_s(SCC0, SC_DIM), 0), (seg_s(SCH0, SC_DIM), 0)]
        tssm = jnp.pad(state_ssm_conv[l], ((0, 0), (8 - (SSM_CONV - 1), 0), (0, 0)))
        tsc = jnp.pad(state_shortconv[l], ((0, 0), (8 - (SC_WIDTH - 1), 0), (0, 0)))
        o_ssm_s, o_sc_s, h_s = ssd_shortconv(srcs_s, 0, bd, 1, state_to_kernel(state_ssm[l].astype(F32)),
                                             tssm, tsc, consts, td)
        take_s = lambda a: a.reshape(bd, CHUNK, -1)[:, :SAMPLE_SLOT].reshape(n_ts, -1)
        o_ssm = jnp.concatenate([o_ssm_p, take_s(o_ssm_s)], axis=0)
        o_sc = jnp.concatenate([o_sc_p, take_s(o_sc_s)], axis=0)

        x = merge_out(o_nsa, o_ssm, o_sc, proj, x, w_branch[l].astype(BF16), w_out[l].astype(BF16))

        if l % 2 == 0:
            i = l // 2
            tm = _pick(x.shape[0], (640, 512, 384, 256, 128))
            x = swiglu_mlp(x, norm_ffn[l], ffn_w_gate[i:i + 1].astype(BF16), ffn_w_up[i:i + 1].astype(BF16),
                           ffn_w_down[i:i + 1].astype(BF16), jnp.zeros((x.shape[0] // tm,), jnp.int32),
                           jnp.ones((x.shape[0], 1), F32), dense=True, tm=tm)
        else:
            i = l // 2
            x = moe_ffn(x, norm_ffn[l], moe_router[i], moe_w_gate[i].astype(BF16), moe_w_up[i].astype(BF16),
                        moe_w_down[i].astype(BF16), tm_moe)

        shp = (N_KV, HEAD_DIM)
        outs["rows_p"].append(rows_pt.reshape(bp, 4, *shp, tp).transpose(0, 4, 1, 2, 3))
        outs["rows_s"].append(rows_s3[:, :td].reshape(bd, td, 4, *shp))
        keep_p = min(WINDOW, tp)
        outs["win_p"].append(win_pt[:, :, tp - keep_p:].reshape(bp, 2, *shp, keep_p).transpose(0, 4, 1, 2, 3))
        win_cat = jnp.concatenate([cache_nsa_window[l].reshape(bd, lw, 2 * KV_DIM), win_s3[:, :td]], axis=1)
        outs["win_s"].append(win_cat[:, td:].reshape(bd, lw, 2, *shp))
        outs["ssm_p"].append(state_from_kernel(h_p))
        outs["ssm_s"].append(state_from_kernel(h_s))
        tail_p = jnp.stack([proj[(b + 1) * tp - (SSM_CONV - 1):(b + 1) * tp] for b in range(bp)])
        outs["sconv_p"].append(tail_p[:, :, XBC0:XBC0 + SSM_CONV_DIM])
        xbc_cat = jnp.concatenate([state_ssm_conv[l], proj_s[:, :td, XBC0:XBC0 + SSM_CONV_DIM]], axis=1)
        outs["sconv_s"].append(xbc_cat[:, td:])
        tail_c = tail_p[:, (SSM_CONV - 1) - (SC_WIDTH - 1):]
        outs["cconv_p"].append(tail_c[:, :, SCC0:SCC0 + SC_DIM] * tail_c[:, :, SCH0:SCH0 + SC_DIM])
        ch_s = proj_s[:, :td, SCC0:SCC0 + SC_DIM] * proj_s[:, :td, SCH0:SCH0 + SC_DIM]
        outs["cconv_s"].append(jnp.concatenate([state_shortconv[l], ch_s], axis=1)[:, td:])

    y_prompt = x[:n_tp].reshape(bp, tp, d)
    y_sample = x[n_tp:].reshape(bd, SAMPLE_SLOT, d)[:, :td]
    st = lambda k: jnp.stack(outs[k])
    return (y_prompt, y_sample, st("rows_p"), st("rows_s"), st("win_p"), st("win_s"), st("ssm_p"), st("ssm_s"),
            st("sconv_p"), st("sconv_s"), st("cconv_p"), st("cconv_s"))
```

```python
import functools
import math

import jax
import jax.numpy as jnp
from jax import lax
from jax.experimental import pallas as pl
from jax.experimental.pallas import tpu as pltpu

F32 = jnp.float32
BF16 = jnp.bfloat16
HIGHEST = lax.Precision.HIGHEST

D_MODEL = 1024
N_HEADS = 8
N_KV = 2
HEAD_DIM = 64
GROUP = N_HEADS // N_KV
CMP_LEN = 32
CMP_STRIDE = 16
SEL_BLOCK = 64
N_SEL = 16
WINDOW = 512
ROPE_THETA = 10000.0
SSM_HEADS = 8
SSM_HEAD_DIM = 64
SSM_INNER = 512
SSM_GROUPS = 2
SSM_STATE = 64
SSM_CONV = 4
SSM_CONV_DIM = 768
SC_DIM = 512
SC_WIDTH = 3
N_BRANCH = 3
BRANCH_DIM = 512
N_EXPERTS = 8
TOP_K = 2
EPS = 1e-6
NEG_INF = -1e30
FORCED_SCORE = 1e4
Q_DIM = 512
KV_DIM = 128

LANES = 128
CHUNK = 128
SAMPLE_SLOT = 8
VMEM_LIMIT = 56 * 1024 * 1024

Q0, Z0, SCB0, KV0, XBC0, SCC0, SCH0, MG0, SM0, N_PROJ = 0, 512, 1024, 1536, 2304, 3072, 3584, 4096, 7168, 7680
GATE_LANE0 = 0
DT_LANE0 = 24
_OQ, _OKV, _OG, _OZ, _OXBC, _ODT, _OSCB, _OSCC, _OSCH, _OMG = 0, 512, 1280, 1304, 1816, 2584, 2592, 3104, 3616, 4128


def _cparams(sem):
    return pltpu.CompilerParams(dimension_semantics=sem, vmem_limit_bytes=VMEM_LIMIT)


def _pick(n, cands):
    for c in cands:
        if n % c == 0:
            return c
    raise ValueError(f"no tile for {n}")


def _rms(x, w):
    ms = jnp.mean(x * x, axis=-1, keepdims=True)
    return (x * lax.rsqrt(ms + EPS)) * w


def _silu(x):
    return x * jax.nn.sigmoid(x)


def _norm_matmul_body(x_ref, g_ref, w_ref, o_ref, xn_ref):
    @pl.when(pl.program_id(1) == 0)
    def _():
        xn_ref[...] = _rms(x_ref[...], g_ref[...]).astype(BF16)

    o_ref[...] = jnp.dot(xn_ref[...], w_ref[...], preferred_element_type=F32)


def norm_matmul(x, g, w):
    t, d = x.shape
    n = w.shape[1]
    tm = _pick(t, (1280, 1024, 640, 512, 384, 256, 128))
    tn = _pick(n, (1536, 1280, 1024, 512, 256, 128))
    return pl.pallas_call(
        _norm_matmul_body,
        grid=(t // tm, n // tn),
        in_specs=[pl.BlockSpec((tm, d), lambda i, j: (i, 0)),
                  pl.BlockSpec((1, d), lambda i, j: (0, 0)),
                  pl.BlockSpec((d, tn), lambda i, j: (0, j))],
        out_specs=pl.BlockSpec((tm, tn), lambda i, j: (i, j)),
        out_shape=jax.ShapeDtypeStruct((t, n), F32),
        scratch_shapes=[pltpu.VMEM((tm, d), BF16)],
        compiler_params=_cparams(("parallel", "arbitrary")),
        name="norm_matmul",
    )(x, g.reshape(1, d), w)


def _prep_body(q_ref, kv_ref, cos_ref, sin_ref, qw_ref, kw_ref, bd_ref, qo_ref, rows_ref, win_ref):
    cos = cos_ref[...]
    sin = sin_ref[...]
    bd = bd_ref[...]
    lane = lax.broadcasted_iota(jnp.int32, cos.shape, 1)
    first_half = (lane % HEAD_DIM) < (HEAD_DIM // 2)

    def norm_rope(x, w):
        ms = jnp.dot(x * x, bd, precision=HIGHEST, preferred_element_type=F32)
        y = (x * lax.rsqrt(ms + EPS)) * w
        partner = jnp.where(first_half, pltpu.roll(y, LANES - HEAD_DIM // 2, 1), pltpu.roll(y, HEAD_DIM // 2, 1))
        return y * cos + partner * sin

    scale = HEAD_DIM ** -0.5
    for c in range(Q_DIM // LANES):
        sl = slice(c * LANES, (c + 1) * LANES)
        qo_ref[:, sl] = (norm_rope(q_ref[:, sl], qw_ref[...]) * scale).astype(BF16)
    for br in range(3):
        k = norm_rope(kv_ref[:, (2 * br) * LANES:(2 * br + 1) * LANES], kw_ref[br:br + 1, :])
        v = kv_ref[:, (2 * br + 1) * LANES:(2 * br + 2) * LANES]
        if br < 2:
            rows_ref[:, (2 * br) * LANES:(2 * br + 1) * LANES] = k
            rows_ref[:, (2 * br + 1) * LANES:(2 * br + 2) * LANES] = v
        else:
            win_ref[:, 0:LANES] = k
            win_ref[:, LANES:2 * LANES] = v


def nsa_prep(proj, cos, sin, qw, kw, bd):
    t = proj.shape[0]
    tm = _pick(t, (640, 512, 384, 256, 128))
    return pl.pallas_call(
        _prep_body,
        grid=(t // tm,),
        in_specs=[pl.BlockSpec((tm, Q_DIM), lambda i: (i, Q0 // Q_DIM)),
                  pl.BlockSpec((tm, 6 * KV_DIM), lambda i: (i, KV0 // (6 * KV_DIM))),
                  pl.BlockSpec((tm, LANES), lambda i: (i, 0)),
                  pl.BlockSpec((tm, LANES), lambda i: (i, 0)),
                  pl.BlockSpec((1, LANES), lambda i: (0, 0)),
                  pl.BlockSpec((8, LANES), lambda i: (0, 0)),
                  pl.BlockSpec((LANES, LANES), lambda i: (0, 0))],
        out_specs=[pl.BlockSpec((tm, Q_DIM), lambda i: (i, 0)),
                   pl.BlockSpec((tm, 4 * KV_DIM), lambda i: (i, 0)),
                   pl.BlockSpec((tm, 2 * KV_DIM), lambda i: (i, 0))],
        out_shape=[jax.ShapeDtypeStruct((t, Q_DIM), BF16),
                   jax.ShapeDtypeStruct((t, 4 * KV_DIM), F32),
                   jax.ShapeDtypeStruct((t, 2 * KV_DIM), F32)],
        compiler_params=_cparams(("parallel",)),
        name="nsa_prep",
    )(proj, proj, cos, sin, qw, kw, bd)


def _prep_prompt_body(q_ref, kv_ref, cos_ref, sin_ref, qw_ref, kw_ref, bd_ref,
                      qo_ref, cmp_ref, ks_ref, kwo_ref, vst_ref, vwt_ref, rowst_ref, wint_ref, *, tm):
    cos = cos_ref[...]
    sin = sin_ref[...]
    bd = bd_ref[...]
    lane = lax.broadcasted_iota(jnp.int32, cos.shape, 1)
    first_half = (lane % HEAD_DIM) < (HEAD_DIM // 2)

    def norm_rope(x, w):
        ms = jnp.dot(x * x, bd, precision=HIGHEST, preferred_element_type=F32)
        y = (x * lax.rsqrt(ms + EPS)) * w
        partner = jnp.where(first_half, pltpu.roll(y, LANES - HEAD_DIM // 2, 1), pltpu.roll(y, HEAD_DIM // 2, 1))
        return y * cos + partner * sin

    scale = HEAD_DIM ** -0.5
    for c in range(Q_DIM // LANES):
        sl = slice(c * LANES, (c + 1) * LANES)
        qo_ref[:, sl] = (norm_rope(q_ref[:, sl], qw_ref[...]) * scale).astype(BF16)
    for a in range(6):
        x = kv_ref[:, a * LANES:(a + 1) * LANES]
        if a % 2 == 0:
            x = norm_rope(x, kw_ref[a // 2:a // 2 + 1, :])
        xt = x.T
        if a < 4:
            rowst_ref[0, a * LANES:(a + 1) * LANES, :] = xt
        else:
            wint_ref[0, (a - 4) * LANES:(a - 3) * LANES, :] = xt
        if a < 2:
            cmp_ref[:, a * LANES:(a + 1) * LANES] = x
        elif a == 2:
            ks_ref[...] = x.astype(BF16)
        elif a == 4:
            kwo_ref[...] = x.astype(BF16)
        else:
            dst = vst_ref if a == 3 else vwt_ref
            for c in range(tm // CHUNK):
                dst[0, c] = xt[:, c * CHUNK:(c + 1) * CHUNK].astype(BF16)


def nsa_prep_prompt(proj, nb, tp, cos, sin, qw, kw, bd):
    tm = _pick(tp, (512, 256, 128))
    nj = tp // tm
    row = lambda b, j: (b * nj + j, 0)
    return pl.pallas_call(
        functools.partial(_prep_prompt_body, tm=tm),
        grid=(nb, nj),
        in_specs=[pl.BlockSpec((tm, Q_DIM), lambda b, j: (b * nj + j, Q0 // Q_DIM)),
                  pl.BlockSpec((tm, 6 * KV_DIM), lambda b, j: (b * nj + j, KV0 // (6 * KV_DIM))),
                  pl.BlockSpec((tm, LANES), lambda b, j: (j, 0)),
                  pl.BlockSpec((tm, LANES), lambda b, j: (j, 0)),
                  pl.BlockSpec((1, LANES), lambda b, j: (0, 0)),
                  pl.BlockSpec((8, LANES), lambda b, j: (0, 0)),
                  pl.BlockSpec((LANES, LANES), lambda b, j: (0, 0))],
        out_specs=[pl.BlockSpec((tm, Q_DIM), row),
                   pl.BlockSpec((tm, 2 * LANES), row),
                   pl.BlockSpec((tm, LANES), row),
                   pl.BlockSpec((tm, LANES), row),
                   pl.BlockSpec((1, tm // CHUNK, LANES, LANES), lambda b, j: (b, j, 0, 0)),
                   pl.BlockSpec((1, tm // CHUNK, LANES, LANES), lambda b, j: (b, j, 0, 0)),
                   pl.BlockSpec((1, 4 * KV_DIM, tm), lambda b, j: (b, 0, j)),
                   pl.BlockSpec((1, 2 * KV_DIM, tm), lambda b, j: (b, 0, j))],
        out_shape=[jax.ShapeDtypeStruct((nb * tp, Q_DIM), BF16),
                   jax.ShapeDtypeStruct((nb * tp, 2 * LANES), F32),
                   jax.ShapeDtypeStruct((nb * tp, LANES), BF16),
                   jax.ShapeDtypeStruct((nb * tp, LANES), BF16),
                   jax.ShapeDtypeStruct((nb, tp // CHUNK, LANES, LANES), BF16),
                   jax.ShapeDtypeStruct((nb, tp // CHUNK, LANES, LANES), BF16),
                   jax.ShapeDtypeStruct((nb, 4 * KV_DIM, tp), F32),
                   jax.ShapeDtypeStruct((nb, 2 * KV_DIM, tp), F32)],
        compiler_params=_cparams(("parallel", "parallel")),
        name="nsa_prep_prompt",
    )(proj, proj, cos, sin, qw, kw, bd)


def _cmp_body(xk_ref, xv_ref, wk_ref, wv_ref, kc_ref, vct_ref, *, ns):
    ak = jnp.zeros((ns, 2 * LANES), F32)
    av = jnp.zeros((ns, 2 * LANES), F32)
    for j in range(CMP_STRIDE):
        rk = xk_ref[0, pl.ds(j, ns, stride=CMP_STRIDE), :]
        rv = xv_ref[0, pl.ds(j, ns, stride=CMP_STRIDE), :]
        ak = ak + jnp.dot(rk.astype(BF16), wk_ref[j], preferred_element_type=F32)
        av = av + jnp.dot(rv.astype(BF16), wv_ref[j], preferred_element_type=F32)
    kc = ak[:, :LANES] + pltpu.roll(ak[:, LANES:], ns - 1, 0)
    vc = av[:, :LANES] + pltpu.roll(av[:, LANES:], ns - 1, 0)
    kc_ref[0] = kc.astype(BF16)
    vct_ref[0] = vc.T.astype(BF16)


def compress_rows(x3, col_block, wk, wv):
    b, r, _ = x3.shape
    ns = r // CMP_STRIDE
    return pl.pallas_call(
        functools.partial(_cmp_body, ns=ns),
        grid=(b,),
        in_specs=[pl.BlockSpec((1, r, LANES), lambda i: (i, 0, 2 * col_block)),
                  pl.BlockSpec((1, r, LANES), lambda i: (i, 0, 2 * col_block + 1)),
                  pl.BlockSpec((CMP_STRIDE, LANES, 2 * LANES), lambda i: (0, 0, 0)),
                  pl.BlockSpec((CMP_STRIDE, LANES, 2 * LANES), lambda i: (0, 0, 0))],
        out_specs=[pl.BlockSpec((1, ns, LANES), lambda i: (i, 0, 0)),
                   pl.BlockSpec((1, LANES, ns), lambda i: (i, 0, 0))],
        out_shape=[jax.ShapeDtypeStruct((b, ns, LANES), BF16),
                   jax.ShapeDtypeStruct((b, LANES, ns), BF16)],
        compiler_params=_cparams(("parallel",)),
        name="compress_rows",
    )(x3, x3, wk, wv)


def _relayout_body(x_ref, k_ref, vt_ref, *, nchunk):
    for c in range(nchunk):
        x = x_ref[0, c * CHUNK:(c + 1) * CHUNK, :]
        k_ref[0, c * CHUNK:(c + 1) * CHUNK, :] = x[:, :LANES].astype(BF16)
        vt_ref[0, c] = x[:, LANES:].T.astype(BF16)


def kv_relayout(x3, col_block):
    b, r, _ = x3.shape
    rt = _pick(r, (1024, 512, 256, 128))
    nchunk = rt // CHUNK
    return pl.pallas_call(
        functools.partial(_relayout_body, nchunk=nchunk),
        grid=(b, r // rt),
        in_specs=[pl.BlockSpec((1, rt, 2 * LANES), lambda i, j: (i, j, col_block))],
        out_specs=[pl.BlockSpec((1, rt, LANES), lambda i, j: (i, j, 0)),
                   pl.BlockSpec((1, nchunk, LANES, LANES), lambda i, j: (i, j, 0, 0))],
        out_shape=[jax.ShapeDtypeStruct((b, r, LANES), BF16),
                   jax.ShapeDtypeStruct((b, r // CHUNK, LANES, LANES), BF16)],
        compiler_params=_cparams(("parallel", "parallel")),
        name="kv_relayout",
    )(x3)


def _attn_body(*refs, nc, nbp, k_sel, pos0, has_extra, nkc, nwc):
    if has_extra:
        (q_ref, g_ref, kc_ref, vct_ref, st_ref, ks_ref, vst_ref, kw_ref, vwt_ref,
         ksx_ref, vstx_ref, kwx_ref, vwtx_ref, o_ref, gt_ref, sc_ref, sel_ref) = refs
    else:
        (q_ref, g_ref, kc_ref, vct_ref, st_ref, ks_ref, vst_ref, kw_ref, vwt_ref,
         o_ref, gt_ref, sc_ref, sel_ref) = refs
    i = pl.program_id(1)
    lane = lax.broadcasted_iota(jnp.int32, (1, LANES), 1)
    qpos = pos0 + i * CHUNK + lane
    row = lax.broadcasted_iota(jnp.int32, (CHUNK, 1), 0)
    cur_chunk = pos0 // CHUNK + i
    qt = q_ref[0].astype(F32).T.astype(BF16)
    gt_ref[...] = jax.nn.sigmoid(g_ref[0]).T
    half = HEAD_DIM

    def softmax_update(carry, k, vt, qh, mask):
        m, l, acc = carry
        s = jnp.dot(k, qh, preferred_element_type=F32)
        s = jnp.where(mask, s, NEG_INF)
        m_new = jnp.maximum(m, jnp.max(s, axis=0, keepdims=True))
        alpha = jnp.exp(m - m_new)
        p = jnp.where(mask, jnp.exp(s - m_new), 0.0)
        l = alpha * l + jnp.sum(p, axis=0, keepdims=True)
        acc = alpha * acc + jnp.dot(vt, p.astype(BF16), preferred_element_type=F32)
        return m_new, l, acc

    def finish(carry):
        _, l, acc = carry
        return acc * (1.0 / jnp.maximum(l, 1e-30))

    init = (jnp.full((1, LANES), NEG_INF, F32), jnp.zeros((1, LANES), F32), jnp.zeros((half, LANES), F32))
    blk = lax.broadcasted_iota(jnp.int32, (nbp, 1), 0)
    cur_blk = qpos // SEL_BLOCK
    valid = blk <= cur_blk
    forced = (blk == 0) | (blk == cur_blk) | (blk == cur_blk - 1)
    cend = lax.broadcasted_iota(jnp.int32, (nc, 1), 0) * CMP_STRIDE + (CMP_LEN - 1)
    mask_c = cend <= qpos

    for g in range(N_KV):
        gsl = slice(g * half, (g + 1) * half)
        zq = jnp.zeros((half, LANES), BF16)
        qh = []
        for r in range(GROUP):
            qr = qt[(GROUP * g + r) * half:(GROUP * g + r + 1) * half, :]
            qh.append(jnp.concatenate([qr, zq] if g == 0 else [zq, qr], axis=0))
        kc = kc_ref[0]
        vct = vct_ref[0, gsl, :]
        o_c = []
        psum = jnp.zeros((nc, LANES), F32)
        for r in range(GROUP):
            s = jnp.dot(kc, qh[r], preferred_element_type=F32)
            s = jnp.where(mask_c, s, NEG_INF)
            m = jnp.max(s, axis=0, keepdims=True)
            p = jnp.where(mask_c, jnp.exp(s - m), 0.0)
            l = jnp.sum(p, axis=0, keepdims=True)
            p = p * (1.0 / jnp.maximum(l, 1e-30))
            psum = psum + p
            o_c.append(jnp.dot(vct, p.astype(BF16), preferred_element_type=F32))
        imp = jnp.dot(st_ref[...], psum, precision=HIGHEST, preferred_element_type=F32)
        score = jnp.where(valid, jnp.where(forced, FORCED_SCORE, imp), -1.0)
        sc_ref[...] = score

        def rank_step(j, rank):
            rj = sc_ref[pl.ds(j, 1), :]
            ahead = (rj > score) | ((rj == score) & (j < blk))
            return rank + jnp.where(ahead, 1.0, 0.0)

        rank = lax.fori_loop(0, nbp, rank_step, jnp.zeros((nbp, LANES), F32))
        sel_ref[...] = jnp.where(rank < k_sel, 1.0, 0.0)

        def sel_mask(c):
            s0 = sel_ref[pl.ds(2 * c, 1), :]
            s1 = sel_ref[pl.ds(2 * c + 1, 1), :]
            chosen = jnp.where(row < SEL_BLOCK, s0, s1) > 0.5
            kpos = c * CHUNK + row
            return chosen & (kpos <= qpos)

        def sel_step(c, carry):
            k = ks_ref[0, pl.ds(pl.multiple_of(c * CHUNK, CHUNK), CHUNK), :]
            vt = vst_ref[0, c, gsl, :]
            mask = sel_mask(c)
            return tuple(softmax_update(carry[r], k, vt, qh[r], mask) for r in range(GROUP))

        n_loop = nkc if has_extra else i + 1
        carry = lax.fori_loop(0, n_loop, sel_step, tuple(init for _ in range(GROUP)))
        if has_extra:
            mask = sel_mask(nkc)
            k = ksx_ref[0]
            vt = vstx_ref[0, 0, gsl, :]
            carry = tuple(softmax_update(carry[r], k, vt, qh[r], mask) for r in range(GROUP))
        o_s = [finish(carry[r]) for r in range(GROUP)]

        def win_mask(c):
            dpos = qpos - (c * CHUNK + row)
            return (dpos >= 0) & (dpos < WINDOW)

        if has_extra:
            def win_step(lc, carry):
                c = cur_chunk - nwc + lc
                k = kw_ref[0, pl.ds(pl.multiple_of(lc * CHUNK, CHUNK), CHUNK), :]
                vt = vwt_ref[0, lc, gsl, :]
                mask = win_mask(c)
                return tuple(softmax_update(carry[r], k, vt, qh[r], mask) for r in range(GROUP))

            carry = lax.fori_loop(0, nwc, win_step, tuple(init for _ in range(GROUP)))
            mask = win_mask(cur_chunk)
            k = kwx_ref[0]
            vt = vwtx_ref[0, 0, gsl, :]
            carry = tuple(softmax_update(carry[r], k, vt, qh[r], mask) for r in range(GROUP))
        else:
            def win_step(c, carry):
                k = kw_ref[0, pl.ds(pl.multiple_of(c * CHUNK, CHUNK), CHUNK), :]
                vt = vwt_ref[0, c, gsl, :]
                mask = win_mask(c)
                return tuple(softmax_update(carry[r], k, vt, qh[r], mask) for r in range(GROUP))

            first = jnp.maximum(i - WINDOW // CHUNK, 0)
            carry = lax.fori_loop(first, i + 1, win_step, tuple(init for _ in range(GROUP)))
        o_w = [finish(carry[r]) for r in range(GROUP)]

        for r in range(GROUP):
            h = GROUP * g + r
            o = (gt_ref[pl.ds(3 * h, 1), :] * o_c[r] + gt_ref[pl.ds(3 * h + 1, 1), :] * o_s[r]
                 + gt_ref[pl.ds(3 * h + 2, 1), :] * o_w[r])
            if r % 2 == 0:
                pending = o
            else:
                pair = jnp.concatenate([pending, o], axis=0).T
                o_ref[0, :, (h // 2) * LANES:(h // 2 + 1) * LANES] = pair.astype(BF16)


def nsa_attention(q3, g3, kc, vct, st, ks, vst, kw, vwt, extra, *, pos0, n_blk):
    b, tq, _ = q3.shape
    nq = tq // CHUNK
    nc = kc.shape[1]
    nbp = st.shape[0]
    nkc = vst.shape[1]
    nwc = vwt.shape[1]
    has_extra = extra is not None
    per_b3 = lambda i, j: (i, 0, 0)
    per_b4 = lambda i, j: (i, 0, 0, 0)
    in_specs = [pl.BlockSpec((1, CHUNK, Q_DIM), lambda i, j: (i, j, 0)),
                pl.BlockSpec((1, CHUNK, LANES), lambda i, j: (i, j, 0)),
                pl.BlockSpec((1, nc, LANES), per_b3),
                pl.BlockSpec((1, LANES, nc), per_b3),
                pl.BlockSpec((nbp, nc), lambda i, j: (0, 0)),
                pl.BlockSpec((1, nkc * CHUNK, LANES), per_b3),
                pl.BlockSpec((1, nkc, LANES, LANES), per_b4),
                pl.BlockSpec((1, nwc * CHUNK, LANES), per_b3),
                pl.BlockSpec((1, nwc, LANES, LANES), per_b4)]
    args = [q3, g3, kc, vct, st, ks, vst, kw, vwt]
    if has_extra:
        in_specs += [pl.BlockSpec((1, CHUNK, LANES), per_b3), pl.BlockSpec((1, 1, LANES, LANES), per_b4),
                     pl.BlockSpec((1, CHUNK, LANES), per_b3), pl.BlockSpec((1, 1, LANES, LANES), per_b4)]
        args += list(extra)
    body = functools.partial(_attn_body, nc=nc, nbp=nbp, k_sel=min(N_SEL, n_blk), pos0=pos0,
                             has_extra=has_extra, nkc=nkc, nwc=nwc)
    return pl.pallas_call(
        body,
        grid=(b, nq),
        in_specs=in_specs,
        out_specs=pl.BlockSpec((1, CHUNK, Q_DIM), lambda i, j: (i, j, 0)),
        out_shape=jax.ShapeDtypeStruct((b, tq, Q_DIM), BF16),
        scratch_shapes=[pltpu.VMEM((LANES, LANES), F32), pltpu.VMEM((nbp, LANES), F32), pltpu.VMEM((nbp, LANES), F32)],
        compiler_params=_cparams(("parallel", "arbitrary")),
        name="nsa_attention",
    )(*args)


KB = 512
WIN_KEYS = WINDOW + CHUNK
SAMPLE_QL = 32


def _tile_lanes(x, n):
    return x if n == 1 else jnp.concatenate([x] * n, axis=1)


def _flash_update(carry, s, vt, bias, ht):
    m, l, acc = carry
    s = s + _tile_lanes(bias, ht)
    m_new = jnp.maximum(m, jnp.max(s, axis=0, keepdims=True))
    alpha = jnp.exp(m - m_new)
    p = jnp.exp(s - m_new)
    l = alpha * l + jnp.sum(p, axis=0, keepdims=True)
    acc = alpha * acc + jnp.dot(vt, p.astype(BF16), preferred_element_type=F32)
    return m_new, l, acc


def _flash_init(width):
    return (jnp.full((1, width), NEG_INF, F32), jnp.zeros((1, width), F32), jnp.zeros((HEAD_DIM, width), F32))


def _flash_finish(carry):
    _, l, acc = carry
    return acc * (1.0 / l)


def _compressed_and_select(qg, kc, vct, st, qpos, sc_ref, sel_ref, *, nc, nbp, k_sel, ht, head_sum):
    cend = lax.broadcasted_iota(jnp.int32, (nc, 1), 0) * CMP_STRIDE + (CMP_LEN - 1)
    bias_c = jnp.where(cend <= qpos, 0.0, NEG_INF)
    s = jnp.dot(kc, qg, preferred_element_type=F32) + _tile_lanes(bias_c, ht)
    m = jnp.max(s, axis=0, keepdims=True)
    p = jnp.exp(s - m)
    l = jnp.sum(p, axis=0, keepdims=True)
    seen = jnp.where(qpos >= CMP_LEN - 1, 1.0, 0.0)
    p = p * (_tile_lanes(seen, ht) / l)
    o_c = jnp.dot(vct, p.astype(BF16), preferred_element_type=F32)
    psum = head_sum(p)
    imp = jnp.dot(st, psum, precision=HIGHEST, preferred_element_type=F32)
    blk = lax.broadcasted_iota(jnp.int32, (nbp, 1), 0)
    cur_blk = qpos // SEL_BLOCK
    valid = blk <= cur_blk
    forced = (blk == 0) | (blk == cur_blk) | (blk == cur_blk - 1)
    score = jnp.where(valid, jnp.where(forced, FORCED_SCORE, imp), -1.0)
    bias = jnp.full((nbp, LANES), NEG_INF, F32)
    blkf = blk.astype(F32)
    for _ in range(k_sel):
        top = jnp.max(score, axis=0, keepdims=True)
        first = jnp.min(jnp.where(score == top, blkf, float(nbp)), axis=0, keepdims=True)
        pick = blkf == first
        bias = jnp.where(pick, 0.0, bias)
        score = jnp.where(pick, -2.0, score)
    sel_ref[...] = bias
    return o_c


def _block_bias(sel_ref, first_blk, nblk):
    rows = [jnp.broadcast_to(sel_ref[pl.ds(first_blk + j, 1), :], (SEL_BLOCK, LANES)) for j in range(nblk)]
    return jnp.concatenate(rows, axis=0)


def _attn_prompt_body(q_ref, g_ref, kc_ref, vct_ref, st_ref, ks_ref, vst_ref, kw_ref, vwt_ref,
                      o_ref, gt_ref, sc_ref, sel_ref, *, nc, nbp, k_sel):
    i = pl.program_id(1)
    ht = GROUP
    kpc = KB // CHUNK
    lane = lax.broadcasted_iota(jnp.int32, (1, LANES), 1)
    qpos = i * CHUNK + lane
    qt = q_ref[0].astype(F32).T.astype(BF16)
    gt_ref[...] = jax.nn.sigmoid(g_ref[...]).T
    half = HEAD_DIM
    zq = jnp.zeros((half, ht * LANES), BF16)
    head_sum = lambda p: p[:, 0:LANES] + p[:, LANES:2 * LANES] + p[:, 2 * LANES:3 * LANES] + p[:, 3 * LANES:]
    krow = lax.broadcasted_iota(jnp.int32, (KB, 1), 0)
    wrow = lax.broadcasted_iota(jnp.int32, (WIN_KEYS, 1), 0)

    for g in range(N_KV):
        gsl = slice(g * half, (g + 1) * half)
        qrow = jnp.concatenate([qt[(GROUP * g + r) * half:(GROUP * g + r + 1) * half, :] for r in range(GROUP)], axis=1)
        qg = jnp.concatenate([qrow, zq] if g == 0 else [zq, qrow], axis=0)
        o_c = _compressed_and_select(qg, kc_ref[0], vct_ref[0, gsl, :], st_ref[...], qpos, sc_ref, sel_ref,
                                     nc=nc, nbp=nbp, k_sel=k_sel, ht=ht, head_sum=head_sum)

        def sel_tile(c):
            k = ks_ref[0, pl.ds(pl.multiple_of(c * KB, KB), KB), :]
            s = jnp.dot(k, qg, preferred_element_type=F32)
            vt = jnp.concatenate([vst_ref[0, kpc * c + j, gsl, :] for j in range(kpc)], axis=1)
            return s, vt, _block_bias(sel_ref, (KB // SEL_BLOCK) * c, KB // SEL_BLOCK)

        def sel_step(c, carry):
            s, vt, bias = sel_tile(c)
            return _flash_update(carry, s, vt, bias, ht)

        last = i // kpc
        carry = lax.fori_loop(0, last, sel_step, _flash_init(ht * LANES))
        s, vt, bias = sel_tile(last)
        bias = jnp.where(last * KB + krow <= qpos, bias, NEG_INF)
        o_s = _flash_finish(_flash_update(carry, s, vt, bias, ht))

        cs = jnp.maximum(i - WINDOW // CHUNK, 0)
        kwin = kw_ref[0, pl.ds(pl.multiple_of(cs * CHUNK, CHUNK), WIN_KEYS), :]
        s = jnp.dot(kwin, qg, preferred_element_type=F32)
        vt = jnp.concatenate([vwt_ref[0, cs + j, gsl, :] for j in range(WIN_KEYS // CHUNK)], axis=1)
        dpos = qpos - (cs * CHUNK + wrow)
        bias = jnp.where((dpos >= 0) & (dpos < WINDOW), 0.0, NEG_INF)
        o_w = _flash_finish(_flash_update(_flash_init(ht * LANES), s, vt, bias, ht))

        for r in range(GROUP):
            h = GROUP * g + r
            lsl = slice(r * LANES, (r + 1) * LANES)
            o = (gt_ref[pl.ds(3 * h, 1), :] * o_c[:, lsl] + gt_ref[pl.ds(3 * h + 1, 1), :] * o_s[:, lsl]
                 + gt_ref[pl.ds(3 * h + 2, 1), :] * o_w[:, lsl])
            if r % 2 == 0:
                pending = o
            else:
                pair = jnp.concatenate([pending, o], axis=0).T
                o_ref[0, :, (h // 2) * LANES:(h // 2 + 1) * LANES] = pair.astype(BF16)


def nsa_attention_prompt(q3, proj, kc, vct, st, ks, vst, kw, vwt, *, n_blk):
    b, tq, _ = q3.shape
    assert tq % KB == 0 and tq >= WIN_KEYS
    nq = tq // CHUNK
    nc = kc.shape[1]
    nbp = st.shape[0]
    nkc = vst.shape[1]
    per_b3 = lambda i, j: (i, 0, 0)
    per_b4 = lambda i, j: (i, 0, 0, 0)
    body = functools.partial(_attn_prompt_body, nc=nc, nbp=nbp, k_sel=min(N_SEL, n_blk))
    return pl.pallas_call(
        body,
        grid=(b, nq),
        in_specs=[pl.BlockSpec((1, CHUNK, Q_DIM), lambda i, j: (i, j, 0)),
                  pl.BlockSpec((CHUNK, LANES), lambda i, j: (i * nq + j, SM0 // LANES)),
                  pl.BlockSpec((1, nc, LANES), per_b3),
                  pl.BlockSpec((1, LANES, nc), per_b3),
                  pl.BlockSpec((nbp, nc), lambda i, j: (0, 0)),
                  pl.BlockSpec((1, nkc * CHUNK, LANES), per_b3),
                  pl.BlockSpec((1, nkc, LANES, LANES), per_b4),
                  pl.BlockSpec((1, nkc * CHUNK, LANES), per_b3),
                  pl.BlockSpec((1, nkc, LANES, LANES), per_b4)],
        out_specs=pl.BlockSpec((1, CHUNK, Q_DIM), lambda i, j: (i, j, 0)),
        out_shape=jax.ShapeDtypeStruct((b, tq, Q_DIM), BF16),
        scratch_shapes=[pltpu.VMEM((LANES, LANES), F32), pltpu.VMEM((nbp, LANES), F32), pltpu.VMEM((nbp, LANES), F32)],
        compiler_params=_cparams(("parallel", "arbitrary")),
        name="nsa_attention_prompt",
    )(q3, proj, kc, vct, st, ks, vst, kw, vwt)


def _page_copies(pages_ref, pt_ref, buf, sem, layer, seq, slot, row0, npages):
    return [pltpu.make_async_copy(pages_ref.at[layer, pt_ref[seq, p], pl.ds(row0, 2 * LANES), :],
                                  buf.at[slot, p], sem.at[slot]) for p in range(npages)]


def _fetch_pages(pages_ref, pt_ref, buf, sem, layer, row0, npages):
    b = pl.program_id(0)
    slot = lax.rem(b, 2)

    @pl.when(b == 0)
    def _():
        for cp in _page_copies(pages_ref, pt_ref, buf, sem, layer, 0, 0, row0, npages):
            cp.start()

    @pl.when(b + 1 < pl.num_programs(0))
    def _():
        for cp in _page_copies(pages_ref, pt_ref, buf, sem, layer, b + 1, 1 - slot, row0, npages):
            cp.start()

    for cp in _page_copies(pages_ref, pt_ref, buf, sem, layer, b, slot, row0, npages):
        cp.wait()
    return slot


def _cmp_sample_body(pt_ref, pages_ref, wk_ref, wv_ref, kc_ref, vct_ref, buf, sem, xk_ref, xv_ref, *, npages, layer):
    slot = _fetch_pages(pages_ref, pt_ref, buf, sem, layer, 0, npages)
    for p in range(npages):
        xk_ref[p * CHUNK:(p + 1) * CHUNK, :] = buf[slot, p, 0:LANES, :].T
        xv_ref[p * CHUNK:(p + 1) * CHUNK, :] = buf[slot, p, LANES:2 * LANES, :].T
    ns = npages * CHUNK // CMP_STRIDE
    ak = jnp.zeros((ns, 2 * LANES), F32)
    av = jnp.zeros((ns, 2 * LANES), F32)
    for j in range(CMP_STRIDE):
        rk = xk_ref[pl.ds(j, ns, stride=CMP_STRIDE), :]
        rv = xv_ref[pl.ds(j, ns, stride=CMP_STRIDE), :]
        ak = ak + jnp.dot(rk.astype(BF16), wk_ref[j], preferred_element_type=F32)
        av = av + jnp.dot(rv.astype(BF16), wv_ref[j], preferred_element_type=F32)
    kc = ak[:, :LANES] + pltpu.roll(ak[:, LANES:], ns - 1, 0)
    vc = av[:, :LANES] + pltpu.roll(av[:, LANES:], ns - 1, 0)
    kc_ref[0] = kc.astype(BF16)
    vct_ref[0] = vc.T.astype(BF16)


def compress_pages(pages_t, page_table, layer, wk, wv):
    nb, npages = page_table.shape
    ns = npages * CHUNK // CMP_STRIDE
    grid_spec = pltpu.PrefetchScalarGridSpec(
        num_scalar_prefetch=1,
        grid=(nb,),
        in_specs=[pl.BlockSpec(memory_space=pl.ANY),
                  pl.BlockSpec((CMP_STRIDE, LANES, 2 * LANES), lambda i, pt: (0, 0, 0)),
                  pl.BlockSpec((CMP_STRIDE, LANES, 2 * LANES), lambda i, pt: (0, 0, 0))],
        out_specs=[pl.BlockSpec((1, ns, LANES), lambda i, pt: (i, 0, 0)),
                   pl.BlockSpec((1, LANES, ns), lambda i, pt: (i, 0, 0))],
        scratch_shapes=[pltpu.VMEM((2, npages, 2 * LANES, LANES), F32), pltpu.SemaphoreType.DMA((2,)),
                        pltpu.VMEM((npages * CHUNK, LANES), F32), pltpu.VMEM((npages * CHUNK, LANES), F32)],
    )
    return pl.pallas_call(
        functools.partial(_cmp_sample_body, npages=npages, layer=layer),
        grid_spec=grid_spec,
        out_shape=[jax.ShapeDtypeStruct((nb, ns, LANES), BF16), jax.ShapeDtypeStruct((nb, LANES, ns), BF16)],
        compiler_params=_cparams(("arbitrary",)),
        name="compress_pages",
    )(page_table, pages_t, wk, wv)


def _attn_sample_body(pt_ref, pages_ref, qg_ref, gs_ref, kc_ref, vct_ref, st_ref, wc_ref, kn_ref, vnt_ref,
                      kwn_ref, vwnt_ref, o_ref, buf, sem, sc_ref, sel_ref, *, npages, layer, nc, nbp, k_sel, pos0):
    slot = _fetch_pages(pages_ref, pt_ref, buf, sem, layer, 2 * LANES, npages)
    half = HEAD_DIM
    kpc = KB // CHUNK
    lane = lax.broadcasted_iota(jnp.int32, (1, LANES), 1)
    qpos = pos0 + lane % SAMPLE_QL
    head_sum = lambda p: (p + pltpu.roll(p, SAMPLE_QL, 1) + pltpu.roll(p, 2 * SAMPLE_QL, 1)
                          + pltpu.roll(p, 3 * SAMPLE_QL, 1))
    qgs = [qg_ref[0, g] for g in range(N_KV)]
    o_c = []
    for g in range(N_KV):
        o_c.append(_compressed_and_select(qgs[g], kc_ref[0], vct_ref[0, g * half:(g + 1) * half, :], st_ref[...],
                                          qpos, sc_ref, sel_ref.at[g], nc=nc, nbp=nbp, k_sel=k_sel, ht=1,
                                          head_sum=head_sum))
    row = lax.broadcasted_iota(jnp.int32, (CHUNK, 1), 0)
    new_vis = (pos0 + row) <= qpos

    def sel_step(c, carry):
        k = jnp.concatenate([buf[slot, kpc * c + j, 0:LANES, :].T for j in range(kpc)], axis=0).astype(BF16)
        out = []
        for g in range(N_KV):
            s = jnp.dot(k, qgs[g], preferred_element_type=F32)
            vt = jnp.concatenate([buf[slot, kpc * c + j, pl.ds(LANES + g * half, half), :] for j in range(kpc)],
                                 axis=1).astype(BF16)
            bias = _block_bias(sel_ref.at[g], (KB // SEL_BLOCK) * c, KB // SEL_BLOCK)
            out.append(_flash_update(carry[g], s, vt, bias, 1))
        return tuple(out)

    carry = lax.fori_loop(0, npages // kpc, sel_step, tuple(_flash_init(LANES) for _ in range(N_KV)), unroll=2)
    o_s = []
    for g in range(N_KV):
        s = jnp.dot(kn_ref[0], qgs[g], preferred_element_type=F32)
        bias = jnp.where(new_vis, _block_bias(sel_ref.at[g], npages * CHUNK // SEL_BLOCK, CHUNK // SEL_BLOCK), NEG_INF)
        o_s.append(_flash_finish(_flash_update(carry[g], s, vnt_ref[0, g * half:(g + 1) * half, :], bias, 1)))

    lw = wc_ref.shape[3]
    kw = jnp.concatenate([wc_ref[0, 0, 0:LANES, j * CHUNK:(j + 1) * CHUNK].T for j in range(lw // CHUNK)],
                         axis=0).astype(BF16)
    wrow = lax.broadcasted_iota(jnp.int32, (lw, 1), 0)
    bias_w = jnp.where(qpos - (pos0 - lw + wrow) < WINDOW, 0.0, NEG_INF)
    bias_n = jnp.where(new_vis, 0.0, NEG_INF)
    for g in range(N_KV):
        gsl = slice(g * half, (g + 1) * half)
        s = jnp.dot(kw, qgs[g], preferred_element_type=F32)
        vt = wc_ref[0, 0, pl.ds(LANES + g * half, half), :].astype(BF16)
        carry_w = _flash_update(_flash_init(LANES), s, vt, bias_w, 1)
        s = jnp.dot(kwn_ref[0], qgs[g], preferred_element_type=F32)
        o_w = _flash_finish(_flash_update(carry_w, s, vwnt_ref[0, gsl, :], bias_n, 1))
        gates = jax.nn.sigmoid(gs_ref[0, g])
        o_ref[0, g] = gates[0:1, :] * o_c[g] + gates[1:2, :] * o_s[g] + gates[2:3, :] * o_w


def nsa_attention_sample(pages_t, wcache_t, page_table, layer, qg, gs, kc, vct, st, kn, vnt, kwn, vwnt, *, pos0, n_blk):
    nb, npages = page_table.shape
    lw = wcache_t.shape[3]
    nc = kc.shape[1]
    nbp = st.shape[0]
    assert npages % (KB // CHUNK) == 0 and pos0 == npages * CHUNK and lw == WINDOW
    per_b3 = lambda i, pt: (i, 0, 0)
    per_b4 = lambda i, pt: (i, 0, 0, 0)
    grid_spec = pltpu.PrefetchScalarGridSpec(
        num_scalar_prefetch=1,
        grid=(nb,),
        in_specs=[pl.BlockSpec(memory_space=pl.ANY),
                  pl.BlockSpec((1, N_KV, LANES, LANES), per_b4),
                  pl.BlockSpec((1, N_KV, 8, LANES), per_b4),
                  pl.BlockSpec((1, nc, LANES), per_b3),
                  pl.BlockSpec((1, LANES, nc), per_b3),
                  pl.BlockSpec((nbp, nc), lambda i, pt: (0, 0)),
                  pl.BlockSpec((1, 1, 2 * LANES, lw), lambda i, pt: (layer, i, 0, 0)),
                  pl.BlockSpec((1, CHUNK, LANES), per_b3), pl.BlockSpec((1, LANES, CHUNK), per_b3),
                  pl.BlockSpec((1, CHUNK, LANES), per_b3), pl.BlockSpec((1, LANES, CHUNK), per_b3)],
        out_specs=pl.BlockSpec((1, N_KV, HEAD_DIM, LANES), per_b4),
        scratch_shapes=[pltpu.VMEM((2, npages, 2 * LANES, LANES), F32), pltpu.SemaphoreType.DMA((2,)),
                        pltpu.VMEM((nbp, LANES), F32), pltpu.VMEM((N_KV, nbp, LANES), F32)],
    )
    body = functools.partial(_attn_sample_body, npages=npages, layer=layer, nc=nc, nbp=nbp,
                             k_sel=min(N_SEL, n_blk), pos0=pos0)
    return pl.pallas_call(
        body,
        grid_spec=grid_spec,
        out_shape=jax.ShapeDtypeStruct((nb, N_KV, HEAD_DIM, LANES), F32),
        compiler_params=_cparams(("arbitrary",)),
        name="nsa_attention_sample",
    )(page_table, pages_t, qg, gs, kc, vct, st, wcache_t, kn, vnt, kwn, vwnt)


def _bc_body(xbc_ref, z_ref, sm_ref, scb_ref, scc_ref, sch_ref, h0_ref, tssm_ref, tsc_ref,
             cw_ref, cb_ref, dtb_ref, a_ref, dw_ref, nw_ref, scw_ref, ex_ref, ext_ref, tri_ref, trit_ref,
             ossm_ref, osc_ref, ht_ref, xp_ref, cp_ref, h_ref, *, valid_len):
    c = pl.program_id(1)
    nchunks = pl.num_programs(1)
    L = CHUNK

    @pl.when(c == 0)
    def _():
        xp_ref[0:8, :] = tssm_ref[0]
        cp_ref[0:8, :] = tsc_ref[0]
        h_ref[...] = h0_ref[0]

    x = xbc_ref[...]
    xp_ref[8:8 + L, :] = x
    conv = cb_ref[...] + cw_ref[0:1, :] * xp_ref[pl.ds(8 - 3, L), :]
    for k in range(1, SSM_CONV):
        conv = conv + cw_ref[k:k + 1, :] * xp_ref[pl.ds(8 - 3 + k, L), :]
    xp_ref[0:8, :] = x[L - 8:L, :]
    xbc = _silu(conv)
    xs = xbc[:, :SSM_INNER]
    bm = xbc[:, SSM_INNER:SSM_INNER + LANES]
    cm = xbc[:, SSM_INNER + LANES:]

    ch = scc_ref[...] * sch_ref[...]
    cp_ref[8:8 + L, :] = ch
    sc = scw_ref[0:1, :] * cp_ref[pl.ds(8 - 2, L), :]
    for k in range(1, SC_WIDTH):
        sc = sc + scw_ref[k:k + 1, :] * cp_ref[pl.ds(8 - 2 + k, L), :]
    cp_ref[0:8, :] = ch[L - 8:L, :]
    osc_ref[...] = (scb_ref[...] * sc).astype(osc_ref.dtype)

    raw = sm_ref[...] + dtb_ref[...]
    dt = jnp.maximum(raw, 0.0) + jnp.log1p(jnp.exp(-jnp.abs(raw)))
    if valid_len < L:
        trow = lax.broadcasted_iota(jnp.int32, (L, 1), 0)
        dt = jnp.where(trow < valid_len, dt, 0.0)
    da = dt * a_ref[...]
    acum = jnp.dot(tri_ref[...], da, precision=HIGHEST, preferred_element_type=F32)
    acum_t = jnp.dot(da.T, trit_ref[...], precision=HIGHEST, preferred_element_type=F32)
    ex = ex_ref[...]
    dt_w = jnp.dot(dt, ex, precision=HIGHEST, preferred_element_type=F32)
    acum_w = jnp.dot(acum, ex, precision=HIGHEST, preferred_element_type=F32)
    last_w = acum_w[L - 1:L, :]
    xdt = xs * dt_w
    xdt_b = xdt.astype(BF16)
    xtail_t = (xdt * jnp.exp(last_w - acum_w)).T.astype(BF16)
    eacum_w = jnp.exp(acum_w)
    acum_wt = jnp.dot(ext_ref[...], acum_t, precision=HIGHEST, preferred_element_type=F32)
    hdecay = jnp.exp(acum_wt[:, L - 1:L])

    lane = lax.broadcasted_iota(jnp.int32, (1, LANES), 1)
    trow2 = lax.broadcasted_iota(jnp.int32, (L, L), 0)
    tcol2 = lax.broadcasted_iota(jnp.int32, (L, L), 1)
    causal = tcol2 <= trow2
    ys = []
    heads_per_group = SSM_HEADS // SSM_GROUPS
    for g in range(SSM_GROUPS):
        gmask = (lane // SSM_STATE) == g
        cm_g = jnp.where(gmask, cm, 0.0).astype(BF16)
        bm_g = jnp.where(gmask, bm, 0.0).astype(BF16)
        cb = lax.dot_general(cm_g, bm_g, (((1,), (1,)), ((), ())), preferred_element_type=F32)
        for pair in range(heads_per_group // 2):
            pcol = g * (heads_per_group // 2) + pair
            psl = slice(pcol * LANES, (pcol + 1) * LANES)
            xpair = xdt_b[:, psl]
            y = jnp.zeros((L, LANES), F32)
            for e in range(2):
                h = 2 * pcol + e
                ac = acum[:, DT_LANE0 + h:DT_LANE0 + h + 1]
                ar = acum_t[DT_LANE0 + h:DT_LANE0 + h + 1, :]
                decay = jnp.exp(jnp.where(causal, ac - ar, NEG_INF))
                emask = (lane // SSM_HEAD_DIM) == e
                xh = jnp.where(emask, xpair, jnp.zeros_like(xpair))
                y = y + jnp.dot((cb * decay).astype(BF16), xh, preferred_element_type=F32)
            hp = h_ref[pcol * LANES:(pcol + 1) * LANES, :]
            y_off = lax.dot_general(cm_g, hp.astype(BF16), (((1,), (1,)), ((), ())), preferred_element_type=F32)
            ys.append(y + y_off * eacum_w[:, psl])
        rsl = slice(g * heads_per_group * SSM_HEAD_DIM, (g + 1) * heads_per_group * SSM_HEAD_DIM)
        upd = jnp.dot(xtail_t[rsl, :], bm_g, preferred_element_type=F32)
        h_ref[rsl, :] = h_ref[rsl, :] * hdecay[rsl, :] + upd

    y = jnp.concatenate(ys, axis=1)
    y = (y + xs * dw_ref[...]) * _silu(z_ref[...])
    ossm_ref[...] = _rms(y, nw_ref[...]).astype(ossm_ref.dtype)

    @pl.when(c == nchunks - 1)
    def _():
        ht_ref[0] = h_ref[...]


def ssd_shortconv(srcs, row_block0, nb, nchunks, h0, tail_ssm, tail_sc, consts, valid_len):
    cw, cb, dtb, a_full, dw, nw, scw, ex, ext, tri, trit = consts
    widths = (SSM_CONV_DIM, SSM_INNER, LANES, SC_DIM, SC_DIM, SC_DIM)
    in_specs, args = [], []
    for (arr, cblk), w in zip(srcs, widths):
        in_specs.append(pl.BlockSpec((CHUNK, w), functools.partial(
            lambda b, c, cblk: (row_block0 + b * nchunks + c, cblk), cblk=cblk)))
        args.append(arr)
    hp = SSM_HEADS * SSM_HEAD_DIM
    in_specs += [pl.BlockSpec((1, hp, LANES), lambda b, c: (b, 0, 0)),
                 pl.BlockSpec((1, 8, SSM_CONV_DIM), lambda b, c: (b, 0, 0)),
                 pl.BlockSpec((1, 8, SC_DIM), lambda b, c: (b, 0, 0))]
    args += [h0, tail_ssm, tail_sc]
    for cst in (cw, cb, dtb, a_full, dw, nw, scw, ex, ext, tri, trit):
        in_specs.append(pl.BlockSpec(cst.shape, lambda b, c: (0, 0)))
        args.append(cst)
    rows = nb * nchunks * CHUNK
    return pl.pallas_call(
        functools.partial(_bc_body, valid_len=valid_len),
        grid=(nb, nchunks),
        in_specs=in_specs,
        out_specs=[pl.BlockSpec((CHUNK, SSM_INNER), lambda b, c: (b * nchunks + c, 0)),
                   pl.BlockSpec((CHUNK, SC_DIM), lambda b, c: (b * nchunks + c, 0)),
                   pl.BlockSpec((1, hp, LANES), lambda b, c: (b, 0, 0))],
        out_shape=[jax.ShapeDtypeStruct((rows, SSM_INNER), BF16),
                   jax.ShapeDtypeStruct((rows, SC_DIM), BF16),
                   jax.ShapeDtypeStruct((nb, hp, LANES), F32)],
        scratch_shapes=[pltpu.VMEM((8 + CHUNK, SSM_CONV_DIM), F32), pltpu.VMEM((8 + CHUNK, SC_DIM), F32),
                        pltpu.VMEM((hp, LANES), F32)],
        compiler_params=_cparams(("parallel", "arbitrary")),
        name="ssd_shortconv",
    )(*args)


def _merge_body(on_ref, os_ref, oc_ref, g0_ref, g1_ref, g2_ref, x_ref, wb_ref, wo_ref, o_ref):
    y = jax.nn.sigmoid(g0_ref[...]) * jnp.dot(on_ref[...], wb_ref[0], preferred_element_type=F32)
    y = y + jax.nn.sigmoid(g1_ref[...]) * jnp.dot(os_ref[...], wb_ref[1], preferred_element_type=F32)
    y = y + jax.nn.sigmoid(g2_ref[...]) * jnp.dot(oc_ref[...], wb_ref[2], preferred_element_type=F32)
    o_ref[...] = x_ref[...] + jnp.dot(y.astype(BF16), wo_ref[...], preferred_element_type=F32)


def merge_out(o_nsa, o_ssm, o_sc, proj, x, wb, wo):
    t, d = x.shape
    tm = _pick(t, (640, 512, 384, 256, 128))
    row = lambda i: (i, 0)
    mg = MG0 // D_MODEL
    return pl.pallas_call(
        _merge_body,
        grid=(t // tm,),
        in_specs=[pl.BlockSpec((tm, BRANCH_DIM), row), pl.BlockSpec((tm, BRANCH_DIM), row),
                  pl.BlockSpec((tm, BRANCH_DIM), row),
                  pl.BlockSpec((tm, d), lambda i: (i, mg)), pl.BlockSpec((tm, d), lambda i: (i, mg + 1)),
                  pl.BlockSpec((tm, d), lambda i: (i, mg + 2)),
                  pl.BlockSpec((tm, d), row),
                  pl.BlockSpec((N_BRANCH, BRANCH_DIM, d), lambda i: (0, 0, 0)),
                  pl.BlockSpec((d, d), lambda i: (0, 0))],
        out_specs=pl.BlockSpec((tm, d), row),
        out_shape=jax.ShapeDtypeStruct((t, d), F32),
        compiler_params=_cparams(("parallel",)),
        name="merge_out",
    )(o_nsa, o_ssm, o_sc, proj, proj, proj, x, wb, wo)


def _mlp_body(te_ref, x_ref, nw_ref, wg_ref, wu_ref, wd_ref, sc_ref, o_ref, xn_ref, acc_ref, *, dense):
    f = pl.program_id(1)

    @pl.when(f == 0)
    def _():
        if dense:
            xn_ref[...] = _rms(x_ref[...], nw_ref[...]).astype(BF16)
        else:
            xn_ref[...] = x_ref[...]
        acc_ref[...] = jnp.zeros_like(acc_ref)

    xn = xn_ref[...]
    gate = jnp.dot(xn, wg_ref[0], preferred_element_type=F32)
    up = jnp.dot(xn, wu_ref[0], preferred_element_type=F32)
    act = (_silu(gate) * up).astype(BF16)
    acc_ref[...] += jnp.dot(act, wd_ref[0], preferred_element_type=F32)

    @pl.when(f == pl.num_programs(1) - 1)
    def _():
        if dense:
            o_ref[...] = x_ref[...] + acc_ref[...]
        else:
            o_ref[...] = sc_ref[...] * acc_ref[...]


def swiglu_mlp(x, nw, wg, wu, wd, tile_expert, row_scale, *, dense, tm):
    r, d = x.shape
    ff = wg.shape[2]
    tf = _pick(ff, (1408, 896, 512, 256, 128))
    grid_spec = pltpu.PrefetchScalarGridSpec(
        num_scalar_prefetch=1,
        grid=(r // tm, ff // tf),
        in_specs=[pl.BlockSpec((tm, d), lambda i, f, te: (i, 0)),
                  pl.BlockSpec((1, d), lambda i, f, te: (0, 0)),
                  pl.BlockSpec((1, d, tf), lambda i, f, te: (te[i], 0, f)),
                  pl.BlockSpec((1, d, tf), lambda i, f, te: (te[i], 0, f)),
                  pl.BlockSpec((1, tf, d), lambda i, f, te: (te[i], f, 0)),
                  pl.BlockSpec((tm, 1), lambda i, f, te: (i, 0))],
        out_specs=pl.BlockSpec((tm, d), lambda i, f, te: (i, 0)),
        scratch_shapes=[pltpu.VMEM((tm, d), BF16), pltpu.VMEM((tm, d), F32)],
    )
    return pl.pallas_call(
        functools.partial(_mlp_body, dense=dense),
        grid_spec=grid_spec,
        out_shape=jax.ShapeDtypeStruct((r, d), F32),
        compiler_params=_cparams(("parallel", "arbitrary")),
        name="swiglu_dense" if dense else "swiglu_grouped",
    )(tile_expert, x, nw.reshape(1, d), wg, wu, wd, row_scale)


def _router_body(x_ref, nw_ref, wr_ref, hn_ref, lg_ref):
    hn = _rms(x_ref[...], nw_ref[...])
    hn_ref[...] = hn.astype(BF16)
    lg_ref[...] = jnp.dot(hn, wr_ref[...], precision=HIGHEST, preferred_element_type=F32)


def moe_router(x, nw, wr_pad):
    t, d = x.shape
    tm = _pick(t, (640, 512, 384, 256, 128))
    return pl.pallas_call(
        _router_body,
        grid=(t // tm,),
        in_specs=[pl.BlockSpec((tm, d), lambda i: (i, 0)), pl.BlockSpec((1, d), lambda i: (0, 0)),
                  pl.BlockSpec((d, LANES), lambda i: (0, 0))],
        out_specs=[pl.BlockSpec((tm, d), lambda i: (i, 0)), pl.BlockSpec((tm, LANES), lambda i: (i, 0))],
        out_shape=[jax.ShapeDtypeStruct((t, d), BF16), jax.ShapeDtypeStruct((t, LANES), F32)],
        compiler_params=_cparams(("parallel",)),
        name="moe_router",
    )(x, nw.reshape(1, d), wr_pad)


def moe_ffn(x, nw, router, wg, wu, wd, tm):
    t, d = x.shape
    ne = router.shape[1]
    hn, logits = moe_router(x, nw, jnp.pad(router, ((0, 0), (0, LANES - ne))))
    top_v, top_i = lax.top_k(logits[:, :ne], TOP_K)
    gate = jax.nn.softmax(top_v, axis=-1)
    npair = t * TOP_K
    e_flat = top_i.reshape(npair)
    onehot = (e_flat[:, None] == jnp.arange(ne)[None, :]).astype(jnp.int32)
    csum = jnp.cumsum(onehot, axis=0)
    counts = csum[-1]
    rank = jnp.take_along_axis(csum, e_flat[:, None], axis=1)[:, 0] - 1
    tiles_e = (counts + tm - 1) // tm
    tile_end = jnp.cumsum(tiles_e)
    tile_start = tile_end - tiles_e
    n_tiles = npair // tm + ne
    dest = tile_start[e_flat] * tm + rank
    tile_ids = jnp.arange(n_tiles)
    tile_expert = jnp.minimum(jnp.sum(tile_ids[:, None] >= tile_end[None, :], axis=1), ne - 1).astype(jnp.int32)
    order = jnp.argsort(e_flat, stable=True)
    cnt_start = jnp.cumsum(counts) - counts
    rows = jnp.arange(n_tiles * tm)
    row_e = jnp.repeat(tile_expert, tm)
    within = rows - jnp.repeat(tile_start[tile_expert], tm) * tm
    row_valid = (within < counts[row_e]) & (jnp.repeat(tile_ids, tm) < tile_end[ne - 1])
    src_pair = order[jnp.clip(cnt_start[row_e] + within, 0, npair - 1)]
    xg = jnp.take(hn, src_pair // TOP_K, axis=0)
    scale = jnp.where(row_valid, gate.reshape(npair)[src_pair], 0.0).astype(F32)[:, None]
    y = swiglu_mlp(xg, nw, wg, wu, wd, tile_expert, scale, dense=False, tm=tm)
    contrib = jnp.take(y, dest, axis=0).reshape(t, TOP_K, d)
    return x + contrib[:, 0] + contrib[:, 1]


def _perm_w_in(w):
    d = w.shape[0]
    seg = lambda o, n: w[:, o:o + n]
    small = jnp.concatenate([seg(_OG, 3 * N_HEADS), seg(_ODT, SSM_HEADS),
                             jnp.zeros((d, LANES - 3 * N_HEADS - SSM_HEADS), w.dtype)], axis=1)
    out = jnp.concatenate([seg(_OQ, Q_DIM), seg(_OZ, SSM_INNER), seg(_OSCB, SC_DIM), seg(_OKV, 6 * KV_DIM),
                           seg(_OXBC, SSM_CONV_DIM), seg(_OSCC, SC_DIM), seg(_OSCH, SC_DIM),
                           seg(_OMG, N_BRANCH * D_MODEL), small,
                           jnp.zeros((d, N_PROJ - SM0 - LANES), w.dtype)], axis=1)
    return out.astype(BF16)


def _cmp_weights(w_cmp_l):
    wh = w_cmp_l.reshape(2, 2, CMP_STRIDE, HEAD_DIM, HEAD_DIM)
    eye = jnp.eye(N_KV, dtype=w_cmp_l.dtype)
    big = jnp.einsum('ahjde,gG->ajgdhGe', wh, eye)
    big = big.reshape(2, CMP_STRIDE, N_KV * HEAD_DIM, 2 * N_KV * HEAD_DIM).astype(BF16)
    return big[0], big[1]


def _rope_tables(pos):
    halfd = HEAD_DIM // 2
    inv_freq = jnp.exp(-math.log(ROPE_THETA) * jnp.arange(halfd, dtype=F32) / halfd)
    ang = pos.astype(F32)[:, None] * inv_freq[None, :]
    cos, sin = jnp.cos(ang), jnp.sin(ang)
    cos_t = jnp.concatenate([cos, cos, cos, cos], axis=1)
    sin_t = jnp.concatenate([-sin, sin, -sin, sin], axis=1)
    return cos_t, sin_t


def _block_sum_matrix(n_blk, nbp, nc):
    ratio = SEL_BLOCK // CMP_STRIDE
    b = jnp.arange(nbp)[:, None]
    c = jnp.arange(nc)[None, :]
    return ((c // ratio == b) & (b < n_blk)).astype(F32)


def _pad_rows(x, n):
    return jnp.pad(x, ((0, 0), (0, n - x.shape[1])) + ((0, 0),) * (x.ndim - 2))


def kernel(x_prompt, x_sample, cache_nsa_pages, cache_nsa_window, state_ssm, state_ssm_conv, state_shortconv, page_table, norm_mix, w_in, q_norm, k_norm, w_cmp, ssm_conv_w, ssm_conv_b, ssm_dt_bias, ssm_a_log, ssm_d, ssm_norm, sc_conv_w, w_branch, w_out, norm_ffn, ffn_w_gate, ffn_w_up, ffn_w_down, moe_router, moe_w_gate, moe_w_up, moe_w_down):
    bp, tp, d = x_prompt.shape
    bd, td, _ = x_sample.shape
    depth = w_in.shape[0]
    page = cache_nsa_pages.shape[2]
    past_len = page_table.shape[1] * page
    n_tp, n_ts = bp * tp, bd * SAMPLE_SLOT
    lw = cache_nsa_window.shape[2]
    assert tp % CHUNK == 0 and past_len % CHUNK == 0 and lw % CHUNK == 0 and td <= SAMPLE_SLOT and lw == WINDOW
    assert td < CMP_STRIDE and past_len % CMP_STRIDE == 0
    nqp = tp // CHUNK

    xs_pad = _pad_rows(x_sample, SAMPLE_SLOT).reshape(n_ts, d)
    x = jnp.concatenate([x_prompt.reshape(n_tp, d), xs_pad], axis=0)

    cos_p, sin_p = _rope_tables(jnp.arange(tp, dtype=jnp.int32))
    cos_s, sin_s = _rope_tables(jnp.tile(past_len + jnp.arange(SAMPLE_SLOT, dtype=jnp.int32), bd))
    pages_t = jnp.transpose(cache_nsa_pages, (0, 1, 3, 4, 5, 2)).reshape(depth, -1, 4 * KV_DIM, page)
    wcache_t = jnp.transpose(cache_nsa_window, (0, 1, 3, 4, 5, 2)).reshape(depth, bd, 2 * KV_DIM, lw)
    lane = jnp.arange(LANES)
    bd_mat = ((lane[:, None] // HEAD_DIM) == (lane[None, :] // HEAD_DIM)).astype(F32) / HEAD_DIM
    n_blk_p = -(-tp // SEL_BLOCK)
    nc_p = tp // CMP_STRIDE
    st_p = _block_sum_matrix(n_blk_p, -(-n_blk_p // 8) * 8, nc_p)
    n_blk_s = -(-(past_len + td) // SEL_BLOCK)
    nc_s = past_len // CMP_STRIDE
    st_s = _block_sum_matrix(n_blk_s, -(-(n_blk_s + 1) // 8) * 8, nc_s)
    hl = jnp.arange(SSM_HEADS)
    ex = jnp.zeros((LANES, SSM_INNER), F32).at[DT_LANE0 + jnp.repeat(hl, SSM_HEAD_DIM), jnp.arange(SSM_INNER)].set(1.0)
    tri = (jnp.arange(CHUNK)[None, :] <= jnp.arange(CHUNK)[:, None]).astype(F32)
    trit = tri.T
    zero_h = jnp.zeros((bp, SSM_HEADS * SSM_HEAD_DIM, LANES), F32)
    zero_tssm = jnp.zeros((bp, 8, SSM_CONV_DIM), F32)
    zero_tsc = jnp.zeros((bp, 8, SC_DIM), F32)
    heads_per_group = SSM_HEADS // SSM_GROUPS

    def lanes_24(v):
        return jnp.zeros((1, LANES), F32).at[0, DT_LANE0:DT_LANE0 + SSM_HEADS].set(v.astype(F32))

    def state_to_kernel(h):
        b = h.shape[0]
        hg = h.reshape(b, SSM_GROUPS, heads_per_group * SSM_HEAD_DIM, SSM_STATE)
        out = jnp.zeros((b, SSM_GROUPS, heads_per_group * SSM_HEAD_DIM, SSM_GROUPS, SSM_STATE), F32)
        for g in range(SSM_GROUPS):
            out = out.at[:, g, :, g, :].set(hg[:, g])
        return out.reshape(b, SSM_HEADS * SSM_HEAD_DIM, LANES)

    def state_from_kernel(hk):
        b = hk.shape[0]
        h5 = hk.reshape(b, SSM_GROUPS, heads_per_group * SSM_HEAD_DIM, SSM_GROUPS, SSM_STATE)
        hg = jnp.stack([h5[:, g, :, g, :] for g in range(SSM_GROUPS)], axis=1)
        return hg.reshape(b, SSM_HEADS, SSM_HEAD_DIM, SSM_STATE)

    outs = {k: [] for k in ("rows_p", "rows_s", "win_p", "win_s", "ssm_p", "ssm_s", "sconv_p", "sconv_s",
                            "cconv_p", "cconv_s")}
    tm_moe = 512
    for l in range(depth):
        proj = norm_matmul(x, norm_mix[l], _perm_w_in(w_in[l]))
        qw = jnp.tile(q_norm[l], 2).reshape(1, LANES)
        kw = jnp.pad(jnp.tile(k_norm[l], (1, 2)), ((0, 5), (0, 0)))
        wk, wv = _cmp_weights(w_cmp[l])

        q_p, cmp_p, ks_p, kw_p, vst_p, vwt_p, rows_pt, win_pt = nsa_prep_prompt(
            proj, bp, tp, cos_p, sin_p, qw, kw, bd_mat)
        kc_p, vct_p = compress_rows(cmp_p.reshape(bp, tp, 2 * LANES), 0, wk, wv)
        o_nsa_p = nsa_attention_prompt(q_p.reshape(bp, tp, Q_DIM), proj, kc_p, vct_p, st_p,
                                       ks_p.reshape(bp, tp, LANES), vst_p, kw_p.reshape(bp, tp, LANES), vwt_p,
                                       n_blk=n_blk_p)

        proj_s = proj[n_tp:]
        q_s, rows_s, win_s = nsa_prep(proj_s, cos_s, sin_s, qw, kw, bd_mat)
        rows_s3 = rows_s.reshape(bd, SAMPLE_SLOT, 4 * KV_DIM)
        win_s3 = win_s.reshape(bd, SAMPLE_SLOT, 2 * KV_DIM)
        tmask = (jnp.arange(SAMPLE_SLOT) < td)[None, :, None]
        new_r = _pad_rows(jnp.where(tmask, rows_s3, 0.0), CHUNK).astype(BF16)
        new_w = _pad_rows(jnp.where(tmask, win_s3, 0.0), CHUNK).astype(BF16)
        kn, vnt = new_r[:, :, 2 * LANES:3 * LANES], jnp.swapaxes(new_r[:, :, 3 * LANES:], 1, 2)
        kwn, vwnt = new_w[:, :, :LANES], jnp.swapaxes(new_w[:, :, LANES:], 1, 2)
        q5 = q_s.reshape(bd, SAMPLE_SLOT, N_KV, GROUP, HEAD_DIM).transpose(0, 2, 4, 3, 1)
        q5 = jnp.pad(q5, ((0, 0),) * 4 + ((0, SAMPLE_QL - SAMPLE_SLOT),)).reshape(bd, N_KV, HEAD_DIM, LANES)
        zq = jnp.zeros_like(q5[:, 0])
        qg = jnp.stack([jnp.concatenate([q5[:, 0], zq], axis=1), jnp.concatenate([zq, q5[:, 1]], axis=1)], axis=1)
        g5 = proj_s[:, SM0:SM0 + 3 * N_HEADS].reshape(bd, SAMPLE_SLOT, N_KV, GROUP, 3).transpose(0, 2, 4, 3, 1)
        g5 = jnp.pad(g5, ((0, 0),) * 4 + ((0, SAMPLE_QL - SAMPLE_SLOT),)).reshape(bd, N_KV, 3, LANES)
        gs = jnp.pad(g5, ((0, 0), (0, 0), (0, 8 - 3), (0, 0)))
        kc_s, vct_s = compress_pages(pages_t, page_table, l, wk, wv)
        o_s = nsa_attention_sample(pages_t, wcache_t, page_table, l, qg, gs, kc_s, vct_s, st_s, kn, vnt, kwn, vwnt,
                                   pos0=past_len, n_blk=n_blk_s)
        o_s = o_s.reshape(bd, N_KV, HEAD_DIM, GROUP, SAMPLE_QL)[..., :SAMPLE_SLOT].transpose(0, 4, 1, 3, 2)
        o_nsa = jnp.concatenate([o_nsa_p.reshape(n_tp, Q_DIM), o_s.reshape(n_ts, Q_DIM).astype(BF16)], axis=0)

        consts = (ssm_conv_w[l], ssm_conv_b[l].reshape(1, -1), lanes_24(ssm_dt_bias[l]),
                  lanes_24(-jnp.exp(ssm_a_log[l].astype(F32))),
                  jnp.repeat(ssm_d[l].astype(F32), SSM_HEAD_DIM).reshape(1, -1), ssm_norm[l].reshape(1, -1),
                  sc_conv_w[l], ex, ex.T, tri, trit)
        col_blocks = (XBC0 // SSM_CONV_DIM, Z0 // SSM_INNER, SM0 // LANES, SCB0 // SC_DIM, SCC0 // SC_DIM,
                      SCH0 // SC_DIM)
        o_ssm_p, o_sc_p, h_p = ssd_shortconv([(proj, cb) for cb in col_blocks], 0, bp, nqp,
                                             zero_h, zero_tssm, zero_tsc, consts, CHUNK)
        proj_s = proj_s.reshape(bd, SAMPLE_SLOT, N_PROJ)
        seg_s = lambda o, w: _pad_rows(proj_s[:, :, o:o + w], CHUNK).reshape(bd * CHUNK, w)
        srcs_s = [(seg_s(XBC0, SSM_CONV_DIM), 0), (seg_s(Z0, SSM_INNER), 0), (seg_s(SM0, LANES), 0),
                  (seg_s(SCB0, SC_DIM), 0), (seg_s(SCC0, SC_DIM), 0), (seg_s(SCH0, SC_DIM), 0)]
        tssm = jnp.pad(state_ssm_conv[l], ((0, 0), (8 - (SSM_CONV - 1), 0), (0, 0)))
        tsc = jnp.pad(state_shortconv[l], ((0, 0), (8 - (SC_WIDTH - 1), 0), (0, 0)))
        o_ssm_s, o_sc_s, h_s = ssd_shortconv(srcs_s, 0, bd, 1, state_to_kernel(state_ssm[l].astype(F32)),
                                             tssm, tsc, consts, td)
        take_s = lambda a: a.reshape(bd, CHUNK, -1)[:, :SAMPLE_SLOT].reshape(n_ts, -1)
        o_ssm = jnp.concatenate([o_ssm_p, take_s(o_ssm_s)], axis=0)
        o_sc = jnp.concatenate([o_sc_p, take_s(o_sc_s)], axis=0)

        x = merge_out(o_nsa, o_ssm, o_sc, proj, x, w_branch[l].astype(BF16), w_out[l].astype(BF16))

        if l % 2 == 0:
            i = l // 2
            tm = _pick(x.shape[0], (640, 512, 384, 256, 128))
            x = swiglu_mlp(x, norm_ffn[l], ffn_w_gate[i:i + 1].astype(BF16), ffn_w_up[i:i + 1].astype(BF16),
                           ffn_w_down[i:i + 1].astype(BF16), jnp.zeros((x.shape[0] // tm,), jnp.int32),
                           jnp.ones((x.shape[0], 1), F32), dense=True, tm=tm)
        else:
            i = l // 2
            x = moe_ffn(x, norm_ffn[l], moe_router[i], moe_w_gate[i].astype(BF16), moe_w_up[i].astype(BF16),
                        moe_w_down[i].astype(BF16), tm_moe)

        shp = (N_KV, HEAD_DIM)
        outs["rows_p"].append(rows_pt.reshape(bp, 4, *shp, tp).transpose(0, 4, 1, 2, 3))
        outs["rows_s"].append(rows_s3[:, :td].reshape(bd, td, 4, *shp))
        keep_p = min(WINDOW, tp)
        outs["win_p"].append(win_pt[:, :, tp - keep_p:].reshape(bp, 2, *shp, keep_p).transpose(0, 4, 1, 2, 3))
        win_cat = jnp.concatenate([cache_nsa_window[l].reshape(bd, lw, 2 * KV_DIM), win_s3[:, :td]], axis=1)
        outs["win_s"].append(win_cat[:, td:].reshape(bd, lw, 2, *shp))
        outs["ssm_p"].append(state_from_kernel(h_p))
        outs["ssm_s"].append(state_from_kernel(h_s))
        tail_p = jnp.stack([proj[(b + 1) * tp - (SSM_CONV - 1):(b + 1) * tp] for b in range(bp)])
        outs["sconv_p"].append(tail_p[:, :, XBC0:XBC0 + SSM_CONV_DIM])
        xbc_cat = jnp.concatenate([state_ssm_conv[l], proj_s[:, :td, XBC0:XBC0 + SSM_CONV_DIM]], axis=1)
        outs["sconv_s"].append(xbc_cat[:, td:])
        tail_c = tail_p[:, (SSM_CONV - 1) - (SC_WIDTH - 1):]
        outs["cconv_p"].append(tail_c[:, :, SCC0:SCC0 + SC_DIM] * tail_c[:, :, SCH0:SCH0 + SC_DIM])
        ch_s = proj_s[:, :td, SCC0:SCC0 + SC_DIM] * proj_s[:, :td, SCH0:SCH0 + SC_DIM]
        outs["cconv_s"].append(jnp.concatenate([state_shortconv[l], ch_s], axis=1)[:, td:])

    y_prompt = x[:n_tp].reshape(bp, tp, d)
    y_sample = x[n_tp:].reshape(bd, SAMPLE_SLOT, d)[:, :td]
    st = lambda k: jnp.stack(outs[k])
    return (y_prompt, y_sample, st("rows_p"), st("rows_s"), st("win_p"), st("win_s"), st("ssm_p"), st("ssm_s"),
            st("sconv_p"), st("sconv_s"), st("cconv_p"), st("cconv_s"))
```
